```python
import math
import jax, jax.numpy as jnp
from jax import lax
import numpy as np

D_MODEL = 1024
BATCH = 1
SEQ = 16384
DEPTH = 4
DEC_BATCH = 32
DEC_SEQ = 16
PAST_LEN = 4096

CHUNK = 64
D_MIX = D_MODEL
D_A = D_MIX // 2
G_A = 4
C_A = D_A // G_A
GMLP_BLOCK = 128
D_B = D_MIX - D_A
H_B = 4
DK = D_B // H_B
DV = DK
CONV_W = 4
D_QKV = 3 * D_B
DN_CHUNK = 64
D_FF = -(-8 * D_MODEL // (3 * 256)) * 256
D_IN = 2 * D_A + 4 * D_B + 2 * H_B
EPS = 1e-6

kernel_name = "hymba_gmlp_gated_deltanet_stream_step"


def rms_norm(x, g):
    xf = x.astype(jnp.float32)
    y = xf * lax.rsqrt(jnp.mean(xf * xf, axis=-1, keepdims=True) + EPS)
    return (y * g.astype(jnp.float32)).astype(x.dtype)


def l2norm(x):
    xf = x.astype(jnp.float32)
    return xf * lax.rsqrt(jnp.sum(xf * xf, axis=-1, keepdims=True) + EPS)


def causal_conv(xin, buf, w):
    L = xin.shape[1]
    xp = jnp.concatenate([buf.astype(xin.dtype), xin], axis=1)
    y = xp[:, 0:L] * w[0]
    for i in range(1, CONV_W):
        y = y + xp[:, i:i + L] * w[i]
    return y, xp[:, -(CONV_W - 1):]


def gmlp_spatial(u, v, w_s, b_s, blk):
    B, L, _ = v.shape
    nb = L // blk
    vb = v.reshape(B, nb, blk, G_A, C_A)
    ub = u.reshape(B, nb, blk, G_A, C_A)
    pos = jnp.arange(blk)
    mask = (pos[None, :] // CHUNK) <= (pos[:, None] // CHUNK)
    ws = jnp.where(mask, w_s[:, :blk, :blk], 0)
    s = jnp.einsum('gij,bnjgc->bnigc', ws, vb) + b_s[:, :blk].T[:, :, None]
    return (ub * s).reshape(B, L, D_A)


def gated_delta_chunked(q, k, v, g, beta, S0, chunk):
    B, L, H, _ = q.shape
    nc = L // chunk

    def blocks(t):
        return jnp.swapaxes(t.reshape((B, nc, chunk) + t.shape[2:]), 2, 3)

    q, k, v = blocks(q), blocks(k), blocks(v)
    beta = blocks(beta)
    bcum = jnp.cumsum(blocks(g), axis=-1)
    idx = jnp.arange(chunk)
    diff = bcum[..., :, None] - bcum[..., None, :]
    dec_strict = jnp.exp(jnp.where(idx[:, None] > idx[None, :], diff, -jnp.inf))
    dec_incl = jnp.exp(jnp.where(idx[:, None] >= idx[None, :], diff, -jnp.inf))
    M = beta[..., :, None] * jnp.einsum('bnhid,bnhjd->bnhij', k, k) * dec_strict
    T = M + jnp.eye(chunk, dtype=M.dtype)
    rhs = jnp.concatenate([v * beta[..., None], k * (beta * jnp.exp(bcum))[..., None]], axis=-1)
    sol = lax.linalg.triangular_solve(T, rhs, left_side=True, lower=True)
    u, w = sol[..., :DV], sol[..., DV:]
    attn = jnp.einsum('bnhid,bnhjd->bnhij', q, k) * dec_incl
    q_dec = q * jnp.exp(bcum)[..., None]
    b_last = bcum[..., -1:]
    k_dec = k * jnp.exp(b_last - bcum)[..., None]
    g_last = jnp.exp(b_last[..., 0])
    xs = tuple(jnp.moveaxis(t, 1, 0) for t in (u, w, attn, q_dec, k_dec, g_last))

    def step(S, inp):
        u_c, w_c, a_c, qd, kd, gl = inp
        delta = u_c - jnp.einsum('bhik,bhkv->bhiv', w_c, S)
        o = jnp.einsum('bhik,bhkv->bhiv', qd, S) + jnp.einsum('bhij,bhjv->bhiv', a_c, delta)
        S = gl[..., None, None] * S + jnp.einsum('bhik,bhiv->bhkv', kd, delta)
        return S, o

    S_fin, o = lax.scan(step, S0, xs)
    o = jnp.swapaxes(jnp.moveaxis(o, 0, 1), 2, 3).reshape(B, L, H, DV)
    return o, S_fin


def layer(x, conv_buf, S0, blk, w_in, conv_w, a_log, dt_bias, gdn_norm, gmlp_ws, gmlp_bs,
          gmlp_norm, w_o, norm1, norm2, w_gate, w_up, w_down):
    B, L, _ = x.shape
    h = rms_norm(x, norm1)
    p = h @ w_in
    o1 = 2 * D_A
    o2 = o1 + D_QKV
    o3 = o2 + D_B
    o4 = o3 + H_B
    u_a, v_a, qkv, z, b_raw, a_raw = jnp.split(p, [D_A, o1, o2, o3, o4], axis=-1)
    u_a = jax.nn.gelu(u_a)
    v_a = jax.nn.gelu(v_a)
    y_a = gmlp_spatial(u_a, v_a, gmlp_ws, gmlp_bs, blk)
    y_a = rms_norm(y_a.reshape(B, L, G_A, C_A), gmlp_norm.reshape(G_A, C_A)).reshape(B, L, D_A)
    qkv_c, new_buf = causal_conv(qkv, conv_buf, conv_w)
    qkv_c = jax.nn.silu(qkv_c)
    q, k, v = jnp.split(qkv_c, 3, axis=-1)
    q = l2norm(q.reshape(B, L, H_B, DK)) * (DK ** -0.5)
    k = l2norm(k.reshape(B, L, H_B, DK))
    v = v.reshape(B, L, H_B, DV).astype(jnp.float32)
    beta = jax.nn.sigmoid(b_raw.astype(jnp.float32))
    g = -jnp.exp(a_log.astype(jnp.float32)) * jax.nn.softplus(a_raw.astype(jnp.float32) + dt_bias.astype(jnp.float32))
    chunk = DN_CHUNK if L % DN_CHUNK == 0 else L
    o, S_new = gated_delta_chunked(q, k, v, g, beta, S0.astype(jnp.float32), chunk)
    o = rms_norm(o, gdn_norm) * jax.nn.silu(z.reshape(B, L, H_B, DV).astype(jnp.float32))
    y_b = o.reshape(B, L, D_B).astype(x.dtype)
    x = x + jnp.concatenate([y_a, y_b], axis=-1) @ w_o
    h2 = rms_norm(x, norm2)
    x = x + (jax.nn.silu(h2 @ w_gate) * (h2 @ w_up)) @ w_down
    return x, new_buf, S_new.astype(S0.dtype), v_a


def setup_inputs(seed: int = 0) -> dict:
    key = jax.random.key(seed)
    ks = jax.random.split(key, 24)

    def nrm(k, shape, s):
        return jax.random.normal(k, shape, jnp.float32) * s

    dt = jnp.exp(jax.random.uniform(ks[7], (DEPTH, H_B), jnp.float32, math.log(1e-3), math.log(1e-1)))
    return {
        "x_prompt": nrm(ks[0], (BATCH, SEQ, D_MODEL), 1.0),
        "x_sample": nrm(ks[1], (DEC_BATCH, DEC_SEQ, D_MODEL), 1.0),
        "state_gdn": nrm(ks[2], (DEPTH, DEC_BATCH, H_B, DK, DV), 0.3),
        "state_conv": nrm(ks[3], (DEPTH, DEC_BATCH, CONV_W - 1, D_QKV), 1.0),
        "w_in": nrm(ks[4], (DEPTH, D_MODEL, D_IN), D_MODEL ** -0.5),
        "conv_w": nrm(ks[5], (DEPTH, CONV_W, D_QKV), CONV_W ** -0.5),
        "a_log": jnp.log(jax.random.uniform(ks[6], (DEPTH, H_B), jnp.float32, 1.0, 16.0)),
        "dt_bias": dt + jnp.log(-jnp.expm1(-dt)),
        "gdn_norm": 1.0 + nrm(ks[8], (DEPTH, DV), 0.05),
        "gmlp_ws": nrm(ks[9], (DEPTH, G_A, GMLP_BLOCK, GMLP_BLOCK), GMLP_BLOCK ** -0.5),
        "gmlp_bs": 1.0 + nrm(ks[10], (DEPTH, G_A, GMLP_BLOCK), 0.05),
        "gmlp_norm": 1.0 + nrm(ks[11], (DEPTH, D_A), 0.05),
        "w_o": nrm(ks[12], (DEPTH, D_MIX, D_MODEL), D_MIX ** -0.5),
        "norm1": 1.0 + nrm(ks[13], (DEPTH, D_MODEL), 0.05),
        "norm2": 1.0 + nrm(ks[14], (DEPTH, D_MODEL), 0.05),
        "w_gate": nrm(ks[15], (DEPTH, D_MODEL, D_FF), D_MODEL ** -0.5),
        "w_up": nrm(ks[16], (DEPTH, D_MODEL, D_FF), D_MODEL ** -0.5),
        "w_down": nrm(ks[17], (DEPTH, D_FF, D_MODEL), D_FF ** -0.5),
        "final_norm": 1.0 + nrm(ks[18], (D_MODEL,), 0.05),
    }


def reference(x_prompt, x_sample, state_gdn, state_conv, w_in, conv_w, a_log, dt_bias, gdn_norm,
              gmlp_ws, gmlp_bs, gmlp_norm, w_o, norm1, norm2, w_gate, w_up, w_down, final_norm):
    xp = x_prompt
    xs = x_sample
    bp = xp.shape[0]
    gdn_p, conv_p, gdn_s, conv_s, v_s = [], [], [], [], []
    for l in range(DEPTH):
        params = (w_in[l], conv_w[l], a_log[l], dt_bias[l], gdn_norm[l], gmlp_ws[l], gmlp_bs[l],
                  gmlp_norm[l], w_o[l], norm1[l], norm2[l], w_gate[l], w_up[l], w_down[l])
        buf0 = jnp.zeros((bp, CONV_W - 1, D_QKV), xp.dtype)
        S0 = jnp.zeros((bp, H_B, DK, DV), state_gdn.dtype)
        xp, cb, Sp, _ = layer(xp, buf0, S0, GMLP_BLOCK, *params)
        gdn_p.append(Sp)
        conv_p.append(cb)
        xs, cbs, Ss, va = layer(xs, state_conv[l], state_gdn[l], xs.shape[1], *params)
        gdn_s.append(Ss)
        conv_s.append(cbs)
        v_s.append(va)
    y_prompt = rms_norm(xp, final_norm)
    y_sample = rms_norm(xs, final_norm)
    return (y_prompt, y_sample, jnp.stack(gdn_p), jnp.stack(conv_p), jnp.stack(gdn_s), jnp.stack(conv_s), jnp.stack(v_s))
```

```python
import functools

import jax
import jax.numpy as jnp
from jax import lax
from jax.experimental import pallas as pl
from jax.experimental.pallas import tpu as pltpu

D_MODEL = 1024
D_A = 512
G_A = 4
C_A = D_A // G_A
GMLP_BLOCK = 128
GMLP_CAUSAL = 64
D_B = 512
H_B = 4
DK = D_B // H_B
CONV_W = 4
D_QKV = 3 * D_B
DN_CHUNK = 64
D_FF = 2816
P_MAIN = 2 * D_A + 4 * D_B
EPS = 1e-6

LANES = 128
SUBLANES = 8
PROMPT_TILE = 256
SAMPLE_SEQS = 8
VMEM_LIMIT = 58 * 1024 * 1024

BF16 = jnp.bfloat16
F32 = jnp.float32


def _mm(a, b):
    return jnp.dot(a.astype(BF16), b.astype(BF16), preferred_element_type=F32)


def _mm_nt(a, b):
    return lax.dot_general(a.astype(BF16), b.astype(BF16), (((1,), (1,)), ((), ())),
                           preferred_element_type=F32)


def _rms(x, g):
    return x * lax.rsqrt(jnp.mean(x * x, axis=-1, keepdims=True) + EPS) * g


def _softplus(x):
    return jnp.maximum(x, 0.0) + jnp.log1p(jnp.exp(-jnp.abs(x)))


def _iota(shape, axis):
    return lax.broadcasted_iota(jnp.int32, shape, axis)


def _chunk_cumsum(g, stride, length):
    pos = (_iota(g.shape, 0) // stride) % length
    k = 1
    while k < length:
        g = g + jnp.where(pos >= k, pltpu.roll(g, k * stride, 0), 0.0)
        k *= 2
    return g


def _unit_lower_inverse(m, levels):
    n = m.shape[0]
    eye = (_iota((n, n), 0) == _iota((n, n), 1)).astype(F32)
    r = eye - m
    q = m
    for _ in range(1, levels):
        q = _mm(q, q)
        r = r + _mm(r, q)
    return r


def _in_proj(x, n1_ref, w_in_ref, w_ba_ref):
    hb = _rms(x, n1_ref[...]).astype(BF16)
    p = jnp.dot(hb, w_in_ref[...], preferred_element_type=F32)
    pba = jnp.dot(hb, w_ba_ref[...], preferred_element_type=F32)
    return p, pba


def _group_rms(y, g, width):
    outs = []
    for j in range(y.shape[1] // width):
        sl = slice(j * width, (j + 1) * width)
        outs.append(_rms(y[:, sl], g[:, sl]))
    return jnp.concatenate(outs, axis=1)


def _l2norm_heads(x):
    outs = []
    for h in range(H_B):
        xh = x[:, h * DK:(h + 1) * DK]
        outs.append(xh * lax.rsqrt(jnp.sum(xh * xh, axis=-1, keepdims=True) + EPS))
    return outs


def _gates(pba, alog_ref, dtb_ref):
    beta = jax.nn.sigmoid(pba)
    g = -jnp.exp(alog_ref[...]) * _softplus(pba + dtb_ref[...])
    return beta, g


def _out_ffn(x, y_mix, w_o_ref, n2_ref, wg_ref, wu_ref, wd_ref, fn_ref, final):
    x1 = x + jnp.dot(y_mix.astype(BF16), w_o_ref[...], preferred_element_type=F32)
    h2 = _rms(x1, n2_ref[...]).astype(BF16)
    gate = jnp.dot(h2, wg_ref[...], preferred_element_type=F32)
    up = jnp.dot(h2, wu_ref[...], preferred_element_type=F32)
    act = (jax.nn.silu(gate) * up).astype(BF16)
    x2 = x1 + jnp.dot(act, wd_ref[...], preferred_element_type=F32)
    if final:
        x2 = _rms(x2, fn_ref[...])
    return x2


def _gated_out_norm(o, z, gdnn_ref):
    g = jnp.concatenate([gdnn_ref[...]] * H_B, axis=1)
    return _group_rms(o, g, DK) * jax.nn.silu(z)


def _prompt_kernel(x_ref, w_in_ref, w_ba_ref, cw_ref, alog_ref, dtb_ref, gdnn_ref, ws_ref, bst_ref,
                   gmn_ref, w_o_ref, n1_ref, n2_ref, wg_ref, wu_ref, wd_ref, fn_ref,
                   y_ref, s_out_ref, conv_out_ref, s_scr, prev_scr, o_scr, *, final):
    tm = PROMPT_TILE
    step = pl.program_id(0)

    @pl.when(step == 0)
    def _():
        s_scr[...] = jnp.zeros_like(s_scr)
        prev_scr[...] = jnp.zeros_like(prev_scr)

    x = x_ref[...]
    p, pba = _in_proj(x, n1_ref, w_in_ref, w_ba_ref)

    u_a = jax.nn.gelu(p[:, 0:D_A])
    v_a = jax.nn.gelu(p[:, D_A:2 * D_A])
    bi = _iota((GMLP_BLOCK, GMLP_BLOCK), 0) // GMLP_CAUSAL
    bj = _iota((GMLP_BLOCK, GMLP_BLOCK), 1) // GMLP_CAUSAL
    bst = bst_ref[...]
    y_cols = []
    for g in range(G_A):
        wm = jnp.where(bj <= bi, ws_ref[g], 0.0)
        bias = bst[:, g:g + 1]
        blocks = []
        for r in range(tm // GMLP_BLOCK):
            rows = slice(r * GMLP_BLOCK, (r + 1) * GMLP_BLOCK)
            cols = slice(g * C_A, (g + 1) * C_A)
            s = _mm(wm, v_a[rows, cols]) + bias
            blocks.append(u_a[rows, cols] * s)
        y_cols.append(jnp.concatenate(blocks, axis=0))
    y_a = _group_rms(jnp.concatenate(y_cols, axis=1), gmn_ref[...], C_A)

    qkv = p[:, 2 * D_A:2 * D_A + D_QKV]
    prev = prev_scr[...]
    row8 = _iota((SUBLANES, D_QKV), 0)
    conv = cw_ref[CONV_W - 1:CONV_W, :] * qkv
    for s in range(1, CONV_W):
        rolled = pltpu.roll(qkv, s, 0)
        head = jnp.where(row8 < s, pltpu.roll(prev, s, 0), rolled[0:SUBLANES])
        shifted = jnp.concatenate([head, rolled[SUBLANES:]], axis=0)
        conv = conv + cw_ref[CONV_W - 1 - s:CONV_W - s, :] * shifted
    prev_scr[...] = qkv[tm - SUBLANES:tm]
    conv_out_ref[...] = qkv[tm - SUBLANES:tm]
    qkv_c = jax.nn.silu(conv)
    qn = _l2norm_heads(qkv_c[:, 0:D_B])
    kn = _l2norm_heads(qkv_c[:, D_B:2 * D_B])
    v = qkv_c[:, 2 * D_B:3 * D_B]
    z = p[:, 2 * D_A + D_QKV:P_MAIN]

    beta, g = _gates(pba, alog_ref, dtb_ref)
    bc = _chunk_cumsum(g, 1, DN_CHUNK)
    nchunk = tm // DN_CHUNK
    blast = jnp.concatenate(
        [jnp.broadcast_to(bc[(c + 1) * DN_CHUNK - 1:(c + 1) * DN_CHUNK, :], (DN_CHUNK, LANES))
         for c in range(nchunk)], axis=0)
    bc_t = bc.T
    blast_t = blast.T
    ri = _iota((tm, tm), 0)
    ci = _iota((tm, tm), 1)
    same = (ri // DN_CHUNK) == (ci // DN_CHUNK)
    incl = same & (ri >= ci)
    strict = same & (ri > ci)
    pair_col_half = _iota((DK, LANES), 1) // DN_CHUNK
    zeros_half = jnp.zeros((DN_CHUNK, DK), F32)

    for h in range(H_B):
        beta_c = beta[:, h:h + 1]
        bc_c = bc[:, H_B + h:H_B + h + 1]
        bc_r = bc_t[H_B + h:H_B + h + 1, :]
        diff = bc_c - bc_r
        dec = jnp.exp(jnp.where(incl, diff, -jnp.inf))
        kk = _mm_nt(kn[h], kn[h])
        qk = _mm_nt(qn[h] * (DK ** -0.5), kn[h])
        m = beta_c * kk * jnp.where(strict, dec, 0.0)
        attn = qk * dec
        t_inv = _unit_lower_inverse(m, 6)
        ebc = jnp.exp(bc_c)
        rhs = jnp.concatenate([v[:, h * DK:(h + 1) * DK] * beta_c, kn[h] * (beta_c * ebc)], axis=1)
        uw = _mm(t_inv, rhs)
        u = uw[:, 0:DK]
        w = uw[:, DK:2 * DK]
        qd = qn[h] * (DK ** -0.5) * ebc
        kd_t = kn[h].T * jnp.exp(blast_t[H_B + h:H_B + h + 1, :] - bc_r)

        s_state = s_scr[h]
        for c in range(nchunk):
            rows = slice(c * DN_CHUNK, (c + 1) * DN_CHUNK)
            pair = slice((c // 2) * LANES, (c // 2 + 1) * LANES)
            half = c % 2
            delta = u[rows] - _mm(w[rows], s_state)
            delta_pad = jnp.concatenate([delta, zeros_half] if half == 0 else [zeros_half, delta], axis=0)
            o_c = _mm(qd[rows], s_state) + _mm(attn[rows, pair], delta_pad)
            o_scr[rows, h * DK:(h + 1) * DK] = o_c
            kd_blk = jnp.where(pair_col_half == half, kd_t[:, pair], 0.0)
            last = (c + 1) * DN_CHUNK - 1
            g_last = jnp.exp(bc[last:last + 1, H_B + h:H_B + h + 1])
            s_state = g_last * s_state + _mm(kd_blk, delta_pad)
        s_scr[h] = s_state
        s_out_ref[h] = s_state

    y_b = _gated_out_norm(o_scr[...], z, gdnn_ref)
    y_mix = jnp.concatenate([y_a, y_b], axis=1)
    y_ref[...] = _out_ffn(x, y_mix, w_o_ref, n2_ref, wg_ref, wu_ref, wd_ref, fn_ref, final)


def _sample_kernel(x_ref, s0_ref, cst_ref, w_in_ref, w_ba_ref, cw_ref, alog_ref, dtb_ref, gdnn_ref,
                   ws_ref, bst_ref, gmn_ref, w_o_ref, n1_ref, n2_ref, wg_ref, wu_ref, wd_ref, fn_ref,
                   y_ref, s_out_ref, conv_out_ref, va_ref, *, final, length):
    nb = SAMPLE_SEQS
    tm = nb * length
    x = x_ref[...]
    p, pba = _in_proj(x, n1_ref, w_in_ref, w_ba_ref)

    ri = _iota((tm, tm), 0)
    ci = _iota((tm, tm), 1)
    same = (ri % nb) == (ci % nb)

    u_a = jax.nn.gelu(p[:, 0:D_A])
    v_a = jax.nn.gelu(p[:, D_A:2 * D_A])
    va_ref[...] = v_a
    expand = ((_iota((tm, length), 0) // nb) == _iota((tm, length), 1)).astype(F32)
    bst = bst_ref[...]
    y_cols = []
    for g in range(G_A):
        w_small = ws_ref[g][0:length, 0:length]
        wk = jnp.where(same, _mm_nt(_mm(expand, w_small), expand), 0.0)
        cols = slice(g * C_A, (g + 1) * C_A)
        s = _mm(wk, v_a[:, cols]) + bst[:, g:g + 1]
        y_cols.append(u_a[:, cols] * s)
    y_a = _group_rms(jnp.concatenate(y_cols, axis=1), gmn_ref[...], C_A)

    qkv = p[:, 2 * D_A:2 * D_A + D_QKV]
    hist = (CONV_W - 1) * nb
    xp = jnp.concatenate([cst_ref[...], qkv], axis=0)
    conv = cw_ref[0:1, :] * xp[0:tm]
    for i in range(1, CONV_W):
        conv = conv + cw_ref[i:i + 1, :] * xp[i * nb:i * nb + tm]
    conv_out_ref[...] = xp[tm:tm + hist]
    qkv_c = jax.nn.silu(conv)
    qn = _l2norm_heads(qkv_c[:, 0:D_B])
    kn = _l2norm_heads(qkv_c[:, D_B:2 * D_B])
    v = qkv_c[:, 2 * D_B:3 * D_B]
    z = p[:, 2 * D_A + D_QKV:P_MAIN]

    beta, g = _gates(pba, alog_ref, dtb_ref)
    bc = _chunk_cumsum(g, nb, length)
    blast = jnp.concatenate([bc[tm - nb:tm]] * length, axis=0)
    bc_t = bc.T
    blast_t = blast.T
    incl = same & (ri >= ci)
    strict = same & (ri > ci)
    row_seq = _iota((tm, DK), 0) % nb
    row_seq2 = _iota((2 * tm, DK), 0) % nb
    levels = max(1, (length - 1).bit_length())

    o_heads = []
    for h in range(H_B):
        beta_c = beta[:, h:h + 1]
        bc_c = bc[:, H_B + h:H_B + h + 1]
        bc_r = bc_t[H_B + h:H_B + h + 1, :]
        dec = jnp.exp(jnp.where(incl, bc_c - bc_r, -jnp.inf))
        kk = _mm_nt(kn[h], kn[h])
        qk = _mm_nt(qn[h] * (DK ** -0.5), kn[h])
        m = beta_c * kk * jnp.where(strict, dec, 0.0)
        attn = qk * dec
        t_inv = _unit_lower_inverse(m, levels)
        ebc = jnp.exp(bc_c)
        rhs = jnp.concatenate([v[:, h * DK:(h + 1) * DK] * beta_c, kn[h] * (beta_c * ebc)], axis=1)
        uw = _mm(t_inv, rhs)
        u = uw[:, 0:DK]
        w = uw[:, DK:2 * DK]
        qd = qn[h] * (DK ** -0.5) * ebc
        kd_t = kn[h].T * jnp.exp(blast_t[H_B + h:H_B + h + 1, :] - bc_r)

        wq = jnp.concatenate([w, qd], axis=0)
        acc = jnp.zeros((2 * tm, DK), F32)
        for b in range(nb):
            acc = acc + jnp.where(row_seq2 == b, _mm(wq, s0_ref[b, h]), 0.0)
        delta = u - acc[0:tm]
        o_heads.append(acc[tm:2 * tm] + _mm(attn, delta))
        for b in range(nb):
            g_last = jnp.exp(bc[tm - nb + b:tm - nb + b + 1, H_B + h:H_B + h + 1])
            s_out_ref[b, h] = g_last * s0_ref[b, h] + _mm(kd_t, jnp.where(row_seq == b, delta, 0.0))

    y_b = _gated_out_norm(jnp.concatenate(o_heads, axis=1), z, gdnn_ref)
    y_mix = jnp.concatenate([y_a, y_b], axis=1)
    y_ref[...] = _out_ffn(x, y_mix, w_o_ref, n2_ref, wg_ref, wu_ref, wd_ref, fn_ref, final)


def _layer_spec(shape):
    nd = len(shape)

    def make(layer):
        return pl.BlockSpec((None,) + tuple(shape), lambda i, _l=layer: (_l,) + (0,) * nd,
                            pipeline_mode=pl.Buffered(1))
    return make


_WEIGHT_SHAPES = (
    (D_MODEL, P_MAIN),
    (D_MODEL, LANES),
    (CONV_W, D_QKV),
    (1, LANES),
    (1, LANES),
    (1, DK),
    (G_A, GMLP_BLOCK, GMLP_BLOCK),
    None,
    (1, D_A),
    (D_MODEL, D_MODEL),
    (1, D_MODEL),
    (1, D_MODEL),
    (D_MODEL, D_FF),
    (D_MODEL, D_FF),
    (D_FF, D_MODEL),
)


def _weight_specs(layer, bias_rows):
    specs = []
    for shape in _WEIGHT_SHAPES:
        specs.append(_layer_spec(shape if shape is not None else (bias_rows, LANES))(layer))
    specs.append(pl.BlockSpec((1, D_MODEL), lambda i: (0, 0), pipeline_mode=pl.Buffered(1)))
    return specs


def _prompt_layer(x, weights, layer, final):
    seq = x.shape[0]
    tm = PROMPT_TILE
    assert seq % tm == 0
    out_shape = (
        jax.ShapeDtypeStruct((seq, D_MODEL), F32),
        jax.ShapeDtypeStruct((H_B, DK, DK), F32),
        jax.ShapeDtypeStruct((SUBLANES, D_QKV), F32),
    )
    return pl.pallas_call(
        functools.partial(_prompt_kernel, final=final),
        grid=(seq // tm,),
        in_specs=[pl.BlockSpec((tm, D_MODEL), lambda i: (i, 0))] + _weight_specs(layer, GMLP_BLOCK),
        out_specs=(
            pl.BlockSpec((tm, D_MODEL), lambda i: (i, 0)),
            pl.BlockSpec((H_B, DK, DK), lambda i: (0, 0, 0)),
            pl.BlockSpec((SUBLANES, D_QKV), lambda i: (0, 0)),
        ),
        out_shape=out_shape,
        scratch_shapes=[
            pltpu.VMEM((H_B, DK, DK), F32),
            pltpu.VMEM((SUBLANES, D_QKV), F32),
            pltpu.VMEM((tm, D_B), F32),
        ],
        compiler_params=pltpu.CompilerParams(dimension_semantics=("arbitrary",),
                                             vmem_limit_bytes=VMEM_LIMIT),
        name=f"prompt_layer{layer}",
    )(x, *weights)


def _sample_layer(x, s0, cst, weights, layer, final, length):
    nb = SAMPLE_SEQS
    tm = nb * length
    ntile = x.shape[0] // tm
    hist = (CONV_W - 1) * nb
    out_shape = (
        jax.ShapeDtypeStruct((ntile * tm, D_MODEL), F32),
        jax.ShapeDtypeStruct((ntile * nb, H_B, DK, DK), F32),
        jax.ShapeDtypeStruct((ntile, hist, D_QKV), F32),
        jax.ShapeDtypeStruct((ntile * tm, D_A), F32),
    )
    return pl.pallas_call(
        functools.partial(_sample_kernel, final=final, length=length),
        grid=(ntile,),
        in_specs=[
            pl.BlockSpec((tm, D_MODEL), lambda i: (i, 0)),
            pl.BlockSpec((None, nb, H_B, DK, DK), lambda i, _l=layer: (_l, i, 0, 0, 0)),
            pl.BlockSpec((None, None, hist, D_QKV), lambda i, _l=layer: (_l, i, 0, 0)),
        ] + _weight_specs(layer, tm),
        out_specs=(
            pl.BlockSpec((tm, D_MODEL), lambda i: (i, 0)),
            pl.BlockSpec((nb, H_B, DK, DK), lambda i: (i, 0, 0, 0)),
            pl.BlockSpec((None, hist, D_QKV), lambda i: (i, 0, 0)),
            pl.BlockSpec((tm, D_A), lambda i: (i, 0)),
        ),
        out_shape=out_shape,
        compiler_params=pltpu.CompilerParams(dimension_semantics=("arbitrary",),
                                             vmem_limit_bytes=VMEM_LIMIT),
        name=f"sample_layer{layer}",
    )(x, s0, cst, *weights)


def kernel(x_prompt, x_sample, state_gdn, state_conv, w_in, conv_w, a_log, dt_bias, gdn_norm, gmlp_ws, gmlp_bs, gmlp_norm, w_o, norm1, norm2, w_gate, w_up, w_down, final_norm):
    depth = w_in.shape[0]
    bp, seq, _ = x_prompt.shape
    nseq, length, _ = x_sample.shape
    assert bp == 1 and nseq % SAMPLE_SEQS == 0 and length % DN_CHUNK != 0
    nb = SAMPLE_SEQS
    ntile = nseq // nb

    w_ba = jnp.pad(w_in[:, :, P_MAIN:], ((0, 0), (0, 0), (0, LANES - 2 * H_B))).astype(BF16)
    lane_pad = ((0, 0), (0, 0), (H_B, LANES - 2 * H_B))
    bs_pad = jnp.pad(jnp.swapaxes(gmlp_bs, 1, 2), ((0, 0), (0, 0), (0, LANES - G_A)))

    def weights(bias):
        return (
            w_in[:, :, :P_MAIN].astype(BF16), w_ba, conv_w,
            jnp.pad(a_log[:, None, :], lane_pad), jnp.pad(dt_bias[:, None, :], lane_pad),
            gdn_norm[:, None, :], gmlp_ws, bias, gmlp_norm[:, None, :],
            w_o.astype(BF16), norm1[:, None, :], norm2[:, None, :],
            w_gate.astype(BF16), w_up.astype(BF16), w_down.astype(BF16), final_norm[None, :],
        )

    w_prompt = weights(bs_pad)
    w_sample = (w_prompt[:7] + (jnp.repeat(bs_pad[:, :length], nb, axis=1),) + w_prompt[8:])

    xs = x_sample.reshape(ntile, nb, length, D_MODEL).swapaxes(1, 2).reshape(nseq * length, D_MODEL)
    cst = (state_conv.reshape(depth, ntile, nb, CONV_W - 1, D_QKV).swapaxes(2, 3)
           .reshape(depth, ntile, (CONV_W - 1) * nb, D_QKV))
    xp = x_prompt.reshape(seq, D_MODEL)

    gdn_p, conv_p, gdn_s, conv_s, v_s = [], [], [], [], []
    for l in range(depth):
        final = l == depth - 1
        xp, s_fin, conv_tail = _prompt_layer(xp, w_prompt, l, final)
        gdn_p.append(s_fin[None])
        conv_p.append(conv_tail[None, SUBLANES - (CONV_W - 1):])
        xs, s_new, conv_new, v_a = _sample_layer(xs, state_gdn, cst, w_sample, l, final, length)
        gdn_s.append(s_new)
        conv_s.append(conv_new)
        v_s.append(v_a)

    def unorder(t, width):
        return (t.reshape(t.shape[:-2] + (ntile, length, nb, width)).swapaxes(-2, -3)
                .reshape(t.shape[:-2] + (nseq, length, width)))

    conv_s = (jnp.stack(conv_s).reshape(depth, ntile, CONV_W - 1, nb, D_QKV).swapaxes(2, 3)
              .reshape(depth, nseq, CONV_W - 1, D_QKV))
    return (xp.reshape(1, seq, D_MODEL), unorder(xs, D_MODEL), jnp.stack(gdn_p), jnp.stack(conv_p),
            jnp.stack(gdn_s), conv_s, unorder(jnp.stack(v_s), D_A))
```

```python
import functools

import jax
import jax.numpy as jnp
from jax import lax
from jax.experimental import pallas as pl
from jax.experimental.pallas import tpu as pltpu

D_MODEL = 1024
D_A = 512
G_A = 4
C_A = D_A // G_A
GMLP_BLOCK = 128
GMLP_CAUSAL = 64
D_B = 512
H_B = 4
DK = D_B // H_B
CONV_W = 4
D_QKV = 3 * D_B
DN_CHUNK = 64
D_FF = 2816
P_MAIN = 2 * D_A + 4 * D_B
EPS = 1e-6

LANES = 128
SUBLANES = 8
PROMPT_TILE = 256
SAMPLE_SEQS = 8
VMEM_LIMIT = 58 * 1024 * 1024

BF16 = jnp.bfloat16
F32 = jnp.float32


def _mm(a, b):
    return jnp.dot(a.astype(BF16), b.astype(BF16), preferred_element_type=F32)


def _mm_nt(a, b):
    return lax.dot_general(a.astype(BF16), b.astype(BF16), (((1,), (1,)), ((), ())),
                           preferred_element_type=F32)


def _rms(x, g):
    return x * lax.rsqrt(jnp.mean(x * x, axis=-1, keepdims=True) + EPS) * g


def _softplus(x):
    return jnp.maximum(x, 0.0) + jnp.log1p(jnp.exp(-jnp.abs(x)))


def _iota(shape, axis):
    return lax.broadcasted_iota(jnp.int32, shape, axis)


def _chunk_cumsum(g, stride, length):
    pos = (_iota(g.shape, 0) // stride) % length
    k = 1
    while k < length:
        g = g + jnp.where(pos >= k, pltpu.roll(g, k * stride, 0), 0.0)
        k *= 2
    return g


def _unit_lower_inverse(ms, levels):
    n = ms[0].shape[0]
    eye = (_iota((n, n), 0) == _iota((n, n), 1)).astype(F32)
    rs = [eye - m for m in ms]
    qs = list(ms)
    for _ in range(1, levels):
        qs = [_mm(q, q) for q in qs]
        rs = [r + _mm(r, q) for r, q in zip(rs, qs)]
    return rs


def _in_proj(x, n1_ref, w_in_ref, w_ba_ref):
    hb = _rms(x, n1_ref[...]).astype(BF16)
    p = jnp.dot(hb, w_in_ref[...], preferred_element_type=F32)
    pba = jnp.dot(hb, w_ba_ref[...], preferred_element_type=F32)
    return p, pba


def _group_rms(y, g, width):
    outs = []
    for j in range(y.shape[1] // width):
        sl = slice(j * width, (j + 1) * width)
        outs.append(_rms(y[:, sl], g[:, sl]))
    return jnp.concatenate(outs, axis=1)


def _l2norm_heads(x):
    outs = []
    for h in range(H_B):
        xh = x[:, h * DK:(h + 1) * DK]
        outs.append(xh * lax.rsqrt(jnp.sum(xh * xh, axis=-1, keepdims=True) + EPS))
    return outs


def _gates(pba, alog_ref, dtb_ref):
    beta = jax.nn.sigmoid(pba)
    g = -jnp.exp(alog_ref[...]) * _softplus(pba + dtb_ref[...])
    return beta, g


def _out_ffn(x, y_mix, w_o_ref, n2_ref, wg_ref, wu_ref, wd_ref, fn_ref, final):
    x1 = x + jnp.dot(y_mix.astype(BF16), w_o_ref[...], preferred_element_type=F32)
    h2 = _rms(x1, n2_ref[...]).astype(BF16)
    gate = jnp.dot(h2, wg_ref[...], preferred_element_type=F32)
    up = jnp.dot(h2, wu_ref[...], preferred_element_type=F32)
    act = (jax.nn.silu(gate) * up).astype(BF16)
    x2 = x1 + jnp.dot(act, wd_ref[...], preferred_element_type=F32)
    if final:
        x2 = _rms(x2, fn_ref[...])
    return x2


def _gated_out_norm(o, z, gdnn_ref):
    g = jnp.concatenate([gdnn_ref[...]] * H_B, axis=1)
    return _group_rms(o, g, DK) * jax.nn.silu(z)


def _prompt_kernel(x_ref, w_in_ref, w_ba_ref, cw_ref, alog_ref, dtb_ref, gdnn_ref, ws_ref, bst_ref,
                   gmn_ref, w_o_ref, n1_ref, n2_ref, wg_ref, wu_ref, wd_ref, fn_ref,
                   y_ref, s_out_ref, conv_out_ref, s_scr, prev_scr, o_scr, *, final):
    tm = PROMPT_TILE
    step = pl.program_id(0)

    @pl.when(step == 0)
    def _():
        s_scr[...] = jnp.zeros_like(s_scr)
        prev_scr[...] = jnp.zeros_like(prev_scr)

    x = x_ref[...]
    p, pba = _in_proj(x, n1_ref, w_in_ref, w_ba_ref)

    u_a = jax.nn.gelu(p[:, 0:D_A])
    v_a = jax.nn.gelu(p[:, D_A:2 * D_A])
    bi = _iota((GMLP_BLOCK, GMLP_BLOCK), 0) // GMLP_CAUSAL
    bj = _iota((GMLP_BLOCK, GMLP_BLOCK), 1) // GMLP_CAUSAL
    bst = bst_ref[...]
    y_cols = []
    for g in range(G_A):
        wm = jnp.where(bj <= bi, ws_ref[g], 0.0)
        bias = bst[:, g:g + 1]
        blocks = []
        for r in range(tm // GMLP_BLOCK):
            rows = slice(r * GMLP_BLOCK, (r + 1) * GMLP_BLOCK)
            cols = slice(g * C_A, (g + 1) * C_A)
            s = _mm(wm, v_a[rows, cols]) + bias
            blocks.append(u_a[rows, cols] * s)
        y_cols.append(jnp.concatenate(blocks, axis=0))
    y_a = _group_rms(jnp.concatenate(y_cols, axis=1), gmn_ref[...], C_A)

    qkv = p[:, 2 * D_A:2 * D_A + D_QKV]
    prev = prev_scr[...]
    row8 = _iota((SUBLANES, D_QKV), 0)
    conv = cw_ref[CONV_W - 1:CONV_W, :] * qkv
    for s in range(1, CONV_W):
        rolled = pltpu.roll(qkv, s, 0)
        head = jnp.where(row8 < s, pltpu.roll(prev, s, 0), rolled[0:SUBLANES])
        shifted = jnp.concatenate([head, rolled[SUBLANES:]], axis=0)
        conv = conv + cw_ref[CONV_W - 1 - s:CONV_W - s, :] * shifted
    prev_scr[...] = qkv[tm - SUBLANES:tm]
    conv_out_ref[...] = qkv[tm - SUBLANES:tm]
    qkv_c = jax.nn.silu(conv)
    qn = _l2norm_heads(qkv_c[:, 0:D_B])
    kn = _l2norm_heads(qkv_c[:, D_B:2 * D_B])
    v = qkv_c[:, 2 * D_B:3 * D_B]
    z = p[:, 2 * D_A + D_QKV:P_MAIN]

    beta, g = _gates(pba, alog_ref, dtb_ref)
    bc = _chunk_cumsum(g, 1, DN_CHUNK)
    nchunk = tm // DN_CHUNK
    blast = jnp.concatenate(
        [jnp.broadcast_to(bc[(c + 1) * DN_CHUNK - 1:(c + 1) * DN_CHUNK, :], (DN_CHUNK, LANES))
         for c in range(nchunk)], axis=0)
    bc_t = bc.T
    blast_t = blast.T

    ri = _iota((LANES, LANES), 0)
    ci = _iota((LANES, LANES), 1)
    same = (ri // DN_CHUNK) == (ci // DN_CHUNK)
    incl = same & (ri >= ci)
    strict = same & (ri > ci)
    col_half = ci // DN_CHUNK
    probs = [(h, pr) for h in range(H_B) for pr in range(tm // LANES)]

    def lane(h):
        return slice(H_B + h, H_B + h + 1)

    def rows_of(pr):
        return slice(pr * LANES, (pr + 1) * LANES)

    dec, kk, qk = [], [], []
    for h, pr in probs:
        rw = rows_of(pr)
        dec.append(jnp.exp(jnp.where(incl, bc[rw, lane(h)] - bc_t[lane(h), rw], -jnp.inf)))
        kk.append(_mm_nt(kn[h][rw], kn[h][rw]))
        qk.append(_mm_nt(qn[h][rw] * (DK ** -0.5), kn[h][rw]))
    ms = [beta[rows_of(pr), h:h + 1] * kk_i * jnp.where(strict, dec_i, 0.0)
          for (h, pr), kk_i, dec_i in zip(probs, kk, dec)]
    attn = [qk_i * dec_i for qk_i, dec_i in zip(qk, dec)]
    t_inv = _unit_lower_inverse(ms, 6)
    uw, qd, kd_t = [], [], []
    for (h, pr), t_i in zip(probs, t_inv):
        rw = rows_of(pr)
        beta_c = beta[rw, h:h + 1]
        ebc = jnp.exp(bc[rw, lane(h)])
        rhs = jnp.concatenate([kn[h][rw] * (beta_c * ebc), v[rw, h * DK:(h + 1) * DK] * beta_c], axis=1)
        uw.append(_mm(t_i, rhs))
        qd.append(qn[h][rw] * (DK ** -0.5) * ebc)
        kd_t.append(kn[h][rw].T * jnp.exp(blast_t[lane(h), rw] - bc_t[lane(h), rw]))

    a_neg, b_add, q_eff, o_add = {}, {}, {}, {}
    for i, (h, pr) in enumerate(probs):
        for half in range(2):
            c = 2 * pr + half
            hrows = slice(half * DN_CHUNK, (half + 1) * DN_CHUNK)
            kw = _mm(jnp.where(col_half == half, kd_t[i], 0.0), uw[i])
            aw = _mm(attn[i][hrows], uw[i])
            a_neg[h, c], b_add[h, c] = kw[:, 0:DK], kw[:, DK:2 * DK]
            q_eff[h, c], o_add[h, c] = qd[i][hrows] - aw[:, 0:DK], aw[:, DK:2 * DK]

    s_state = [s_scr[h] for h in range(H_B)]
    for c in range(nchunk):
        last = (c + 1) * DN_CHUNK - 1
        for h in range(H_B):
            s_bf = s_state[h].astype(BF16)
            o_scr[c * DN_CHUNK:(c + 1) * DN_CHUNK, h * DK:(h + 1) * DK] = _mm(q_eff[h, c], s_bf) + o_add[h, c]
            g_last = jnp.exp(bc[last:last + 1, lane(h)])
            s_state[h] = g_last * s_state[h] - _mm(a_neg[h, c], s_bf) + b_add[h, c]
    for h in range(H_B):
        s_scr[h] = s_state[h]
        s_out_ref[h] = s_state[h]

    y_b = _gated_out_norm(o_scr[...], z, gdnn_ref)
    y_mix = jnp.concatenate([y_a, y_b], axis=1)
    y_ref[...] = _out_ffn(x, y_mix, w_o_ref, n2_ref, wg_ref, wu_ref, wd_ref, fn_ref, final)


def _sample_kernel(x_ref, s0_ref, cst_ref, w_in_ref, w_ba_ref, cw_ref, alog_ref, dtb_ref, gdnn_ref,
                   ws_ref, bst_ref, gmn_ref, w_o_ref, n1_ref, n2_ref, wg_ref, wu_ref, wd_ref, fn_ref,
                   y_ref, s_out_ref, conv_out_ref, va_ref, *, final, length):
    nb = SAMPLE_SEQS
    tm = nb * length
    x = x_ref[...]
    p, pba = _in_proj(x, n1_ref, w_in_ref, w_ba_ref)

    ri = _iota((tm, tm), 0)
    ci = _iota((tm, tm), 1)
    same = (ri % nb) == (ci % nb)

    u_a = jax.nn.gelu(p[:, 0:D_A])
    v_a = jax.nn.gelu(p[:, D_A:2 * D_A])
    va_ref[...] = v_a
    expand = ((_iota((tm, length), 0) // nb) == _iota((tm, length), 1)).astype(F32)
    bst = bst_ref[...]
    y_cols = []
    for g in range(G_A):
        w_small = ws_ref[g][0:length, 0:length]
        wk = jnp.where(same, _mm_nt(_mm(expand, w_small), expand), 0.0)
        cols = slice(g * C_A, (g + 1) * C_A)
        s = _mm(wk, v_a[:, cols]) + bst[:, g:g + 1]
        y_cols.append(u_a[:, cols] * s)
    y_a = _group_rms(jnp.concatenate(y_cols, axis=1), gmn_ref[...], C_A)

    qkv = p[:, 2 * D_A:2 * D_A + D_QKV]
    hist = (CONV_W - 1) * nb
    xp = jnp.concatenate([cst_ref[...], qkv], axis=0)
    conv = cw_ref[0:1, :] * xp[0:tm]
    for i in range(1, CONV_W):
        conv = conv + cw_ref[i:i + 1, :] * xp[i * nb:i * nb + tm]
    conv_out_ref[...] = xp[tm:tm + hist]
    qkv_c = jax.nn.silu(conv)
    qn = _l2norm_heads(qkv_c[:, 0:D_B])
    kn = _l2norm_heads(qkv_c[:, D_B:2 * D_B])
    v = qkv_c[:, 2 * D_B:3 * D_B]
    z = p[:, 2 * D_A + D_QKV:P_MAIN]

    beta, g = _gates(pba, alog_ref, dtb_ref)
    bc = _chunk_cumsum(g, nb, length)
    blast = jnp.concatenate([bc[tm - nb:tm]] * length, axis=0)
    bc_t = bc.T
    blast_t = blast.T
    incl = same & (ri >= ci)
    strict = same & (ri > ci)
    row_seq = _iota((tm, DK), 0) % nb
    row_seq2 = _iota((2 * tm, DK), 0) % nb
    levels = max(1, (length - 1).bit_length())

    o_heads = []
    for h in range(H_B):
        beta_c = beta[:, h:h + 1]
        bc_c = bc[:, H_B + h:H_B + h + 1]
        bc_r = bc_t[H_B + h:H_B + h + 1, :]
        dec = jnp.exp(jnp.where(incl, bc_c - bc_r, -jnp.inf))
        kk = _mm_nt(kn[h], kn[h])
        qk = _mm_nt(qn[h] * (DK ** -0.5), kn[h])
        m = beta_c * kk * jnp.where(strict, dec, 0.0)
        attn = qk * dec
        t_inv = _unit_lower_inverse([m], levels)[0]
        ebc = jnp.exp(bc_c)
        rhs = jnp.concatenate([v[:, h * DK:(h + 1) * DK] * beta_c, kn[h] * (beta_c * ebc)], axis=1)
        uw = _mm(t_inv, rhs)
        u = uw[:, 0:DK]
        w = uw[:, DK:2 * DK]
        qd = qn[h] * (DK ** -0.5) * ebc
        kd_t = kn[h].T * jnp.exp(blast_t[H_B + h:H_B + h + 1, :] - bc_r)

        wq = jnp.concatenate([w, qd], axis=0)
        acc = jnp.zeros((2 * tm, DK), F32)
        for b in range(nb):
            acc = acc + jnp.where(row_seq2 == b, _mm(wq, s0_ref[b, h]), 0.0)
        delta = u - acc[0:tm]
        o_heads.append(acc[tm:2 * tm] + _mm(attn, delta))
        for b in range(nb):
            g_last = jnp.exp(bc[tm - nb + b:tm - nb + b + 1, H_B + h:H_B + h + 1])
            s_out_ref[b, h] = g_last * s0_ref[b, h] + _mm(kd_t, jnp.where(row_seq == b, delta, 0.0))

    y_b = _gated_out_norm(jnp.concatenate(o_heads, axis=1), z, gdnn_ref)
    y_mix = jnp.concatenate([y_a, y_b], axis=1)
    y_ref[...] = _out_ffn(x, y_mix, w_o_ref, n2_ref, wg_ref, wu_ref, wd_ref, fn_ref, final)


def _layer_spec(shape):
    nd = len(shape)

    def make(layer):
        return pl.BlockSpec((None,) + tuple(shape), lambda i, _l=layer: (_l,) + (0,) * nd,
                            pipeline_mode=pl.Buffered(1))
    return make


_WEIGHT_SHAPES = (
    (D_MODEL, P_MAIN),
    (D_MODEL, LANES),
    (CONV_W, D_QKV),
    (1, LANES),
    (1, LANES),
    (1, DK),
    (G_A, GMLP_BLOCK, GMLP_BLOCK),
    None,
    (1, D_A),
    (D_MODEL, D_MODEL),
    (1, D_MODEL),
    (1, D_MODEL),
    (D_MODEL, D_FF),
    (D_MODEL, D_FF),
    (D_FF, D_MODEL),
)


def _weight_specs(layer, bias_rows):
    specs = []
    for shape in _WEIGHT_SHAPES:
        specs.append(_layer_spec(shape if shape is not None else (bias_rows, LANES))(layer))
    specs.append(pl.BlockSpec((1, D_MODEL), lambda i: (0, 0), pipeline_mode=pl.Buffered(1)))
    return specs


def _prompt_layer(x, weights, layer, final):
    seq = x.shape[0]
    tm = PROMPT_TILE
    assert seq % tm == 0
    out_shape = (
        jax.ShapeDtypeStruct((seq, D_MODEL), F32),
        jax.ShapeDtypeStruct((H_B, DK, DK), F32),
        jax.ShapeDtypeStruct((SUBLANES, D_QKV), F32),
    )
    return pl.pallas_call(
        functools.partial(_prompt_kernel, final=final),
        grid=(seq // tm,),
        in_specs=[pl.BlockSpec((tm, D_MODEL), lambda i: (i, 0))] + _weight_specs(layer, GMLP_BLOCK),
        out_specs=(
            pl.BlockSpec((tm, D_MODEL), lambda i: (i, 0)),
            pl.BlockSpec((H_B, DK, DK), lambda i: (0, 0, 0)),
            pl.BlockSpec((SUBLANES, D_QKV), lambda i: (0, 0)),
        ),
        out_shape=out_shape,
        scratch_shapes=[
            pltpu.VMEM((H_B, DK, DK), F32),
            pltpu.VMEM((SUBLANES, D_QKV), F32),
            pltpu.VMEM((tm, D_B), F32),
        ],
        compiler_params=pltpu.CompilerParams(dimension_semantics=("arbitrary",),
                                             vmem_limit_bytes=VMEM_LIMIT),
        name=f"prompt_layer{layer}",
    )(x, *weights)


def _sample_layer(x, s0, cst, weights, layer, final, length):
    nb = SAMPLE_SEQS
    tm = nb * length
    ntile = x.shape[0] // tm
    hist = (CONV_W - 1) * nb
    out_shape = (
        jax.ShapeDtypeStruct((ntile * tm, D_MODEL), F32),
        jax.ShapeDtypeStruct((ntile * nb, H_B, DK, DK), F32),
        jax.ShapeDtypeStruct((ntile, hist, D_QKV), F32),
        jax.ShapeDtypeStruct((ntile * tm, D_A), F32),
    )
    return pl.pallas_call(
        functools.partial(_sample_kernel, final=final, length=length),
        grid=(ntile,),
        in_specs=[
            pl.BlockSpec((tm, D_MODEL), lambda i: (i, 0)),
            pl.BlockSpec((None, nb, H_B, DK, DK), lambda i, _l=layer: (_l, i, 0, 0, 0)),
            pl.BlockSpec((None, None, hist, D_QKV), lambda i, _l=layer: (_l, i, 0, 0)),
        ] + _weight_specs(layer, tm),
        out_specs=(
            pl.BlockSpec((tm, D_MODEL), lambda i: (i, 0)),
            pl.BlockSpec((nb, H_B, DK, DK), lambda i: (i, 0, 0, 0)),
            pl.BlockSpec((None, hist, D_QKV), lambda i: (i, 0, 0)),
            pl.BlockSpec((tm, D_A), lambda i: (i, 0)),
        ),
        out_shape=out_shape,
        compiler_params=pltpu.CompilerParams(dimension_semantics=("arbitrary",),
                                             vmem_limit_bytes=VMEM_LIMIT),
        name=f"sample_layer{layer}",
    )(x, s0, cst, *weights)


def kernel(x_prompt, x_sample, state_gdn, state_conv, w_in, conv_w, a_log, dt_bias, gdn_norm, gmlp_ws, gmlp_bs, gmlp_norm, w_o, norm1, norm2, w_gate, w_up, w_down, final_norm):
    depth = w_in.shape[0]
    bp, seq, _ = x_prompt.shape
    nseq, length, _ = x_sample.shape
    assert bp == 1 and nseq % SAMPLE_SEQS == 0 and length % DN_CHUNK != 0
    nb = SAMPLE_SEQS
    ntile = nseq // nb

    w_ba = jnp.pad(w_in[:, :, P_MAIN:], ((0, 0), (0, 0), (0, LANES - 2 * H_B))).astype(BF16)
    lane_pad = ((0, 0), (0, 0), (H_B, LANES - 2 * H_B))
    bs_pad = jnp.pad(jnp.swapaxes(gmlp_bs, 1, 2), ((0, 0), (0, 0), (0, LANES - G_A)))

    def weights(bias):
        return (
            w_in[:, :, :P_MAIN].astype(BF16), w_ba, conv_w,
            jnp.pad(a_log[:, None, :], lane_pad), jnp.pad(dt_bias[:, None, :], lane_pad),
            gdn_norm[:, None, :], gmlp_ws, bias, gmlp_norm[:, None, :],
            w_o.astype(BF16), norm1[:, None, :], norm2[:, None, :],
            w_gate.astype(BF16), w_up.astype(BF16), w_down.astype(BF16), final_norm[None, :],
        )

    w_prompt = weights(bs_pad)
    w_sample = (w_prompt[:7] + (jnp.repeat(bs_pad[:, :length], nb, axis=1),) + w_prompt[8:])

    xs = x_sample.reshape(ntile, nb, length, D_MODEL).swapaxes(1, 2).reshape(nseq * length, D_MODEL)
    cst = (state_conv.reshape(depth, ntile, nb, CONV_W - 1, D_QKV).swapaxes(2, 3)
           .reshape(depth, ntile, (CONV_W - 1) * nb, D_QKV))
    xp = x_prompt.reshape(seq, D_MODEL)

    gdn_p, conv_p, gdn_s, conv_s, v_s = [], [], [], [], []
    for l in range(depth):
        final = l == depth - 1
        xp, s_fin, conv_tail = _prompt_layer(xp, w_prompt, l, final)
        gdn_p.append(s_fin[None])
        conv_p.append(conv_tail[None, SUBLANES - (CONV_W - 1):])
        xs, s_new, conv_new, v_a = _sample_layer(xs, state_gdn, cst, w_sample, l, final, length)
        gdn_s.append(s_new)
        conv_s.append(conv_new)
        v_s.append(v_a)

    def unorder(t, width):
        return (t.reshape(t.shape[:-2] + (ntile, length, nb, width)).swapaxes(-2, -3)
                .reshape(t.shape[:-2] + (nseq, length, width)))

    conv_s = (jnp.stack(conv_s).reshape(depth, ntile, CONV_W - 1, nb, D_QKV).swapaxes(2, 3)
              .reshape(depth, nseq, CONV_W - 1, D_QKV))
    return (xp.reshape(1, seq, D_MODEL), unorder(xs, D_MODEL), jnp.stack(gdn_p), jnp.stack(conv_p),
            jnp.stack(gdn_s), conv_s, unorder(jnp.stack(v_s), D_A))
```

```python
import functools

import jax
import jax.numpy as jnp
from jax import lax
from jax.experimental import pallas as pl
from jax.experimental.pallas import tpu as pltpu

D_MODEL = 1024
D_A = 512
G_A = 4
C_A = D_A // G_A
GMLP_BLOCK = 128
GMLP_CAUSAL = 64
D_B = 512
H_B = 4
DK = D_B // H_B
CONV_W = 4
D_QKV = 3 * D_B
DN_CHUNK = 64
D_FF = 2816
P_MAIN = 2 * D_A + 4 * D_B
EPS = 1e-6

LANES = 128
SUBLANES = 8
PROMPT_TILE = 512
PROMPT_GROUP = 256
FFN_SLAB = 512
SAMPLE_SEQS = 8
VMEM_LIMIT = 58 * 1024 * 1024

BF16 = jnp.bfloat16
F32 = jnp.float32


def _mm(a, b):
    return jnp.dot(a.astype(BF16), b.astype(BF16), preferred_element_type=F32)


def _mm_nt(a, b):
    return lax.dot_general(a.astype(BF16), b.astype(BF16), (((1,), (1,)), ((), ())),
                           preferred_element_type=F32)


def _rms(x, g):
    return x * lax.rsqrt(jnp.mean(x * x, axis=-1, keepdims=True) + EPS) * g


def _softplus(x):
    return jnp.maximum(x, 0.0) + jnp.log1p(jnp.exp(-jnp.abs(x)))


def _iota(shape, axis):
    return lax.broadcasted_iota(jnp.int32, shape, axis)


def _chunk_cumsum(g, stride, length):
    pos = (_iota(g.shape, 0) // stride) % length
    k = 1
    while k < length:
        g = g + jnp.where(pos >= k, pltpu.roll(g, k * stride, 0), 0.0)
        k *= 2
    return g


def _unit_lower_inverse(ms, levels):
    n = ms[0].shape[0]
    eye = (_iota((n, n), 0) == _iota((n, n), 1)).astype(F32)
    rs = [eye - m for m in ms]
    qs = list(ms)
    for _ in range(1, levels):
        qs = [_mm(q, q) for q in qs]
        rs = [r + _mm(r, q) for r, q in zip(rs, qs)]
    return rs


def _in_proj(x, n1_ref, w_in_ref, w_ba_ref):
    hb = _rms(x, n1_ref[...]).astype(BF16)
    p = jnp.dot(hb, w_in_ref[...], preferred_element_type=F32)
    pba = jnp.dot(hb, w_ba_ref[...], preferred_element_type=F32)
    return p, pba


def _group_rms(y, g, width):
    outs = []
    for j in range(y.shape[1] // width):
        sl = slice(j * width, (j + 1) * width)
        outs.append(_rms(y[:, sl], g[:, sl]))
    return jnp.concatenate(outs, axis=1)


def _l2norm_heads(x):
    outs = []
    for h in range(H_B):
        xh = x[:, h * DK:(h + 1) * DK]
        outs.append(xh * lax.rsqrt(jnp.sum(xh * xh, axis=-1, keepdims=True) + EPS))
    return outs


def _gates(pba, alog_ref, dtb_ref):
    beta = jax.nn.sigmoid(pba)
    g = -jnp.exp(alog_ref[...]) * _softplus(pba + dtb_ref[...])
    return beta, g


def _out_ffn(x, y_mix, w_o_ref, n2_ref, wg_ref, wu_ref, wd_ref, fn_ref, final):
    x1 = x + jnp.dot(y_mix.astype(BF16), w_o_ref[...], preferred_element_type=F32)
    h2 = _rms(x1, n2_ref[...]).astype(BF16)
    gate = jnp.dot(h2, wg_ref[...], preferred_element_type=F32)
    up = jnp.dot(h2, wu_ref[...], preferred_element_type=F32)
    act = (jax.nn.silu(gate) * up).astype(BF16)
    x2 = x1 + jnp.dot(act, wd_ref[...], preferred_element_type=F32)
    if final:
        x2 = _rms(x2, fn_ref[...])
    return x2


def _gated_out_norm(o, z, gdnn_ref):
    g = jnp.concatenate([gdnn_ref[...]] * H_B, axis=1)
    return _group_rms(o, g, DK) * jax.nn.silu(z)


def _interleave(*stage_lists):
    lists = [s for s in stage_lists if s]
    done = [0] * len(lists)
    for _ in range(sum(len(s) for s in lists)):
        k = min((i for i in range(len(lists)) if done[i] < len(lists[i])),
                key=lambda i: (done[i] + 1) / len(lists[i]))
        lists[k][done[k]]()
        done[k] += 1


def _proj_stages(ctx, x_ref, rows, n1_ref, w_in_ref, w_ba_ref):
    def norm():
        ctx["hb"] = _rms(x_ref[rows, :], n1_ref[...]).astype(BF16)

    def slab(name, lo):
        def run():
            ctx[name] = jnp.dot(ctx["hb"], w_in_ref[:, lo:lo + D_B], preferred_element_type=F32)
        return run

    def gates():
        ctx["pba"] = jnp.dot(ctx["hb"], w_ba_ref[...], preferred_element_type=F32)

    names = ("u", "v", "q", "k", "vv", "z")
    return [norm] + [slab(n, i * D_B) for i, n in enumerate(names)] + [gates]


def _mix_stages(ctx, rows, cw_ref, alog_ref, dtb_ref, gdnn_ref, ws_ref, bst_ref, gmn_ref,
                s_out_ref, conv_out_ref, s_scr, prev_scr, o_scr):
    tm = rows.size
    nchunk = tm // DN_CHUNK
    probs = [(h, pr) for h in range(H_B) for pr in range(tm // LANES)]
    ri = _iota((LANES, LANES), 0)
    ci = _iota((LANES, LANES), 1)

    def lane(h):
        return slice(H_B + h, H_B + h + 1)

    def rows_of(pr):
        return slice(pr * LANES, (pr + 1) * LANES)

    def gmlp():
        u_a = jax.nn.gelu(ctx["u"])
        v_a = jax.nn.gelu(ctx["v"])
        bi = _iota((GMLP_BLOCK, GMLP_BLOCK), 0) // GMLP_CAUSAL
        bj = _iota((GMLP_BLOCK, GMLP_BLOCK), 1) // GMLP_CAUSAL
        bst = bst_ref[...]
        y_cols = []
        for g in range(G_A):
            wm = jnp.where(bj <= bi, ws_ref[g], 0.0)
            blocks = []
            for r in range(tm // GMLP_BLOCK):
                rw = slice(r * GMLP_BLOCK, (r + 1) * GMLP_BLOCK)
                cols = slice(g * C_A, (g + 1) * C_A)
                blocks.append(u_a[rw, cols] * (_mm(wm, v_a[rw, cols]) + bst[:, g:g + 1]))
            y_cols.append(jnp.concatenate(blocks, axis=0))
        ctx["y_a"] = _group_rms(jnp.concatenate(y_cols, axis=1), gmn_ref[...], C_A)

    def conv():
        row8 = _iota((SUBLANES, D_B), 0)
        outs = []
        for i, name in enumerate(("q", "k", "vv")):
            cols = slice(i * D_B, (i + 1) * D_B)
            x = ctx[name]
            prev = prev_scr[:, cols]
            acc = cw_ref[CONV_W - 1:CONV_W, cols] * x
            for s in range(1, CONV_W):
                rolled = pltpu.roll(x, s, 0)
                head = jnp.where(row8 < s, pltpu.roll(prev, s, 0), rolled[0:SUBLANES])
                shifted = jnp.concatenate([head, rolled[SUBLANES:]], axis=0)
                acc = acc + cw_ref[CONV_W - 1 - s:CONV_W - s, cols] * shifted
            prev_scr[:, cols] = x[tm - SUBLANES:tm]
            conv_out_ref[:, cols] = x[tm - SUBLANES:tm]
            outs.append(jax.nn.silu(acc))
        ctx["qn"] = _l2norm_heads(outs[0])
        ctx["kn"] = _l2norm_heads(outs[1])
        ctx["vc"] = outs[2]

    def gates():
        beta, g = _gates(ctx["pba"], alog_ref, dtb_ref)
        bc = _chunk_cumsum(g, 1, DN_CHUNK)
        blast = jnp.concatenate(
            [jnp.broadcast_to(bc[(c + 1) * DN_CHUNK - 1:(c + 1) * DN_CHUNK, :], (DN_CHUNK, LANES))
             for c in range(nchunk)], axis=0)
        ctx.update(beta=beta, bc=bc, bc_t=bc.T, blast_t=blast.T)

    def chunk_mats():
        same = (ri // DN_CHUNK) == (ci // DN_CHUNK)
        incl = same & (ri >= ci)
        strict = same & (ri > ci)
        bc, bc_t, beta, qn, kn = ctx["bc"], ctx["bc_t"], ctx["beta"], ctx["qn"], ctx["kn"]
        ms, attn = [], []
        for h, pr in probs:
            rw = rows_of(pr)
            dec = jnp.exp(jnp.where(incl, bc[rw, lane(h)] - bc_t[lane(h), rw], -jnp.inf))
            kq = _mm_nt(jnp.concatenate([kn[h][rw], qn[h][rw] * (DK ** -0.5)], axis=0), kn[h][rw])
            ms.append(beta[rw, h:h + 1] * kq[0:LANES] * jnp.where(strict, dec, 0.0))
            attn.append(kq[LANES:2 * LANES] * dec)
        eye = (ri == ci).astype(F32)
        ctx.update(attn=attn, qs=[-m for m in ms], rs=[eye - m for m in ms])

    def inverse_level(last):
        def run():
            rs, qs = [], []
            for r, q in zip(ctx["rs"], ctx["qs"]):
                if last:
                    rs.append(r + _mm(r, q))
                else:
                    rq = _mm(jnp.concatenate([r, q], axis=0), q)
                    rs.append(r + rq[0:LANES])
                    qs.append(rq[LANES:2 * LANES])
            ctx.update(rs=rs, qs=qs)
        return run

    def first_level():
        ctx["qs"] = [_mm(q, q) for q in ctx["qs"]]

    def solve():
        bc, bc_t, beta, qn, kn = ctx["bc"], ctx["bc_t"], ctx["beta"], ctx["qn"], ctx["kn"]
        uw, qd, kd_t = [], [], []
        for (h, pr), t_i in zip(probs, ctx["rs"]):
            rw = rows_of(pr)
            beta_c = beta[rw, h:h + 1]
            ebc = jnp.exp(bc[rw, lane(h)])
            rhs = jnp.concatenate([kn[h][rw] * (beta_c * ebc), ctx["vc"][rw, h * DK:(h + 1) * DK] * beta_c],
                                  axis=1)
            uw.append(_mm(t_i, rhs))
            qd.append(qn[h][rw] * (DK ** -0.5) * ebc)
            kd_t.append(kn[h][rw].T * jnp.exp(ctx["blast_t"][lane(h), rw] - bc_t[lane(h), rw]))
        ctx.update(uw=uw, qd=qd, kd_t=kd_t)

    def chunk_terms():
        col_half = ci // DN_CHUNK
        terms = {}
        for i, (h, pr) in enumerate(probs):
            kd_t = ctx["kd_t"][i]
            lhs = jnp.concatenate([jnp.where(col_half == 0, kd_t, 0.0), jnp.where(col_half == 1, kd_t, 0.0),
                                   ctx["attn"][i]], axis=0)
            prod = _mm(lhs, ctx["uw"][i])
            for half in range(2):
                kw = prod[half * DK:(half + 1) * DK]
                aw = prod[2 * DK + half * DN_CHUNK:2 * DK + (half + 1) * DN_CHUNK]
                q_eff = ctx["qd"][i][half * DN_CHUNK:(half + 1) * DN_CHUNK] - aw[:, 0:DK]
                terms[h, 2 * pr + half] = (jnp.concatenate([q_eff, -kw[:, 0:DK]], axis=0).astype(BF16),
                                           aw[:, DK:2 * DK], kw[:, DK:2 * DK])
        ctx["terms"] = terms
        ctx["s"] = [s_scr[h] for h in range(H_B)]

    def recur(c):
        def run():
            last = (c + 1) * DN_CHUNK - 1
            for h in range(H_B):
                lhs, o_add, b_add = ctx["terms"][h, c]
                prod = _mm(lhs, ctx["s"][h])
                o_scr[c * DN_CHUNK:(c + 1) * DN_CHUNK, h * DK:(h + 1) * DK] = prod[0:DN_CHUNK] + o_add
                g_last = jnp.exp(ctx["bc"][last:last + 1, lane(h)])
                ctx["s"][h] = g_last * ctx["s"][h] + prod[DN_CHUNK:DN_CHUNK + DK] + b_add
        return run

    def finish():
        for h in range(H_B):
            s_scr[h] = ctx["s"][h]
            s_out_ref[h] = ctx["s"][h]
        y_b = _gated_out_norm(o_scr[0:tm, :], ctx["z"], gdnn_ref)
        ctx["y_mix"] = jnp.concatenate([ctx["y_a"], y_b], axis=1).astype(BF16)

    return ([gmlp, conv, gates, chunk_mats, first_level] + [inverse_level(False)] * 4
            + [inverse_level(True), solve, chunk_terms]
            + [recur(c) for c in range(nchunk)] + [finish])


def _ffn_stages(ctx, x_ref, y_ref, rows, w_o_ref, n2_ref, wg_ref, wu_ref, wd_ref, fn_ref, final):
    def out_proj():
        x1 = x_ref[rows, :] + jnp.dot(ctx["y_mix"], w_o_ref[...], preferred_element_type=F32)
        ctx["h2"] = _rms(x1, n2_ref[...]).astype(BF16)
        ctx["acc"] = x1

    def slab(lo, width):
        def run():
            gate = jnp.dot(ctx["h2"], wg_ref[:, lo:lo + width], preferred_element_type=F32)
            up = jnp.dot(ctx["h2"], wu_ref[:, lo:lo + width], preferred_element_type=F32)
            act = (jax.nn.silu(gate) * up).astype(BF16)
            ctx["acc"] = ctx["acc"] + jnp.dot(act, wd_ref[lo:lo + width, :], preferred_element_type=F32)
        return run

    def store():
        x2 = ctx["acc"]
        if final:
            x2 = _rms(x2, fn_ref[...])
        y_ref[rows, :] = x2

    slabs = [slab(lo, min(FFN_SLAB, D_FF - lo)) for lo in range(0, D_FF, FFN_SLAB)]
    return [out_proj] + slabs + [store]


def _prompt_kernel(x_ref, w_in_ref, w_ba_ref, cw_ref, alog_ref, dtb_ref, gdnn_ref, ws_ref, bst_ref,
                   gmn_ref, w_o_ref, n1_ref, n2_ref, wg_ref, wu_ref, wd_ref, fn_ref,
                   y_ref, s_out_ref, conv_out_ref, s_scr, prev_scr, o_scr, *, final):
    step = pl.program_id(0)

    @pl.when(step == 0)
    def _():
        s_scr[...] = jnp.zeros_like(s_scr)
        prev_scr[...] = jnp.zeros_like(prev_scr)

    ngroup = PROMPT_TILE // PROMPT_GROUP
    proj, mix, ffn = [], [], []
    for j in range(ngroup):
        ctx = {}
        rows = pl.ds(j * PROMPT_GROUP, PROMPT_GROUP)
        proj.append(_proj_stages(ctx, x_ref, rows, n1_ref, w_in_ref, w_ba_ref))
        mix.append(_mix_stages(ctx, rows, cw_ref, alog_ref, dtb_ref, gdnn_ref, ws_ref, bst_ref, gmn_ref,
                               s_out_ref, conv_out_ref, s_scr, prev_scr, o_scr))
        ffn.append(_ffn_stages(ctx, x_ref, y_ref, rows, w_o_ref, n2_ref, wg_ref, wu_ref, wd_ref, fn_ref,
                               final))
    _interleave(proj[0])
    for j in range(ngroup):
        _interleave(mix[j], proj[j + 1] if j + 1 < ngroup else None, ffn[j - 1] if j > 0 else None)
    _interleave(ffn[ngroup - 1])


def _sample_kernel(x_ref, s0_ref, cst_ref, w_in_ref, w_ba_ref, cw_ref, alog_ref, dtb_ref, gdnn_ref,
                   ws_ref, bst_ref, gmn_ref, w_o_ref, n1_ref, n2_ref, wg_ref, wu_ref, wd_ref, fn_ref,
                   y_ref, s_out_ref, conv_out_ref, va_ref, *, final, length):
    nb = SAMPLE_SEQS
    tm = nb * length
    x = x_ref[...]
    p, pba = _in_proj(x, n1_ref, w_in_ref, w_ba_ref)

    ri = _iota((tm, tm), 0)
    ci = _iota((tm, tm), 1)
    same = (ri % nb) == (ci % nb)

    u_a = jax.nn.gelu(p[:, 0:D_A])
    v_a = jax.nn.gelu(p[:, D_A:2 * D_A])
    va_ref[...] = v_a
    expand = ((_iota((tm, length), 0) // nb) == _iota((tm, length), 1)).astype(F32)
    bst = bst_ref[...]
    y_cols = []
    for g in range(G_A):
        w_small = ws_ref[g][0:length, 0:length]
        wk = jnp.where(same, _mm_nt(_mm(expand, w_small), expand), 0.0)
        cols = slice(g * C_A, (g + 1) * C_A)
        s = _mm(wk, v_a[:, cols]) + bst[:, g:g + 1]
        y_cols.append(u_a[:, cols] * s)
    y_a = _group_rms(jnp.concatenate(y_cols, axis=1), gmn_ref[...], C_A)

    qkv = p[:, 2 * D_A:2 * D_A + D_QKV]
    hist = (CONV_W - 1) * nb
    xp = jnp.concatenate([cst_ref[...], qkv], axis=0)
    conv = cw_ref[0:1, :] * xp[0:tm]
    for i in range(1, CONV_W):
        conv = conv + cw_ref[i:i + 1, :] * xp[i * nb:i * nb + tm]
    conv_out_ref[...] = xp[tm:tm + hist]
    qkv_c = jax.nn.silu(conv)
    qn = _l2norm_heads(qkv_c[:, 0:D_B])
    kn = _l2norm_heads(qkv_c[:, D_B:2 * D_B])
    v = qkv_c[:, 2 * D_B:3 * D_B]
    z = p[:, 2 * D_A + D_QKV:P_MAIN]

    beta, g = _gates(pba, alog_ref, dtb_ref)
    bc = _chunk_cumsum(g, nb, length)
    blast = jnp.concatenate([bc[tm - nb:tm]] * length, axis=0)
    bc_t = bc.T
    blast_t = blast.T
    incl = same & (ri >= ci)
    strict = same & (ri > ci)
    row_seq = _iota((tm, DK), 0) % nb
    row_seq2 = _iota((2 * tm, DK), 0) % nb
    levels = max(1, (length - 1).bit_length())

    o_heads = []
    for h in range(H_B):
        beta_c = beta[:, h:h + 1]
        bc_c = bc[:, H_B + h:H_B + h + 1]
        bc_r = bc_t[H_B + h:H_B + h + 1, :]
        dec = jnp.exp(jnp.where(incl, bc_c - bc_r, -jnp.inf))
        kk = _mm_nt(kn[h], kn[h])
        qk = _mm_nt(qn[h] * (DK ** -0.5), kn[h])
        m = beta_c * kk * jnp.where(strict, dec, 0.0)
        attn = qk * dec
        t_inv = _unit_lower_inverse([m], levels)[0]
        ebc = jnp.exp(bc_c)
        rhs = jnp.concatenate([v[:, h * DK:(h + 1) * DK] * beta_c, kn[h] * (beta_c * ebc)], axis=1)
        uw = _mm(t_inv, rhs)
        u = uw[:, 0:DK]
        w = uw[:, DK:2 * DK]
        qd = qn[h] * (DK ** -0.5) * ebc
        kd_t = kn[h].T * jnp.exp(blast_t[H_B + h:H_B + h + 1, :] - bc_r)

        wq = jnp.concatenate([w, qd], axis=0)
        acc = jnp.zeros((2 * tm, DK), F32)
        for b in range(nb):
            acc = acc + jnp.where(row_seq2 == b, _mm(wq, s0_ref[b, h]), 0.0)
        delta = u - acc[0:tm]
        o_heads.append(acc[tm:2 * tm] + _mm(attn, delta))
        for b in range(nb):
            g_last = jnp.exp(bc[tm - nb + b:tm - nb + b + 1, H_B + h:H_B + h + 1])
            s_out_ref[b, h] = g_last * s0_ref[b, h] + _mm(kd_t, jnp.where(row_seq == b, delta, 0.0))

    y_b = _gated_out_norm(jnp.concatenate(o_heads, axis=1), z, gdnn_ref)
    y_mix = jnp.concatenate([y_a, y_b], axis=1)
    y_ref[...] = _out_ffn(x, y_mix, w_o_ref, n2_ref, wg_ref, wu_ref, wd_ref, fn_ref, final)


def _layer_spec(shape):
    nd = len(shape)

    def make(layer):
        return pl.BlockSpec((None,) + tuple(shape), lambda i, _l=layer: (_l,) + (0,) * nd,
                            pipeline_mode=pl.Buffered(1))
    return make


_WEIGHT_SHAPES = (
    (D_MODEL, P_MAIN),
    (D_MODEL, LANES),
    (CONV_W, D_QKV),
    (1, LANES),
    (1, LANES),
    (1, DK),
    (G_A, GMLP_BLOCK, GMLP_BLOCK),
    None,
    (1, D_A),
    (D_MODEL, D_MODEL),
    (1, D_MODEL),
    (1, D_MODEL),
    (D_MODEL, D_FF),
    (D_MODEL, D_FF),
    (D_FF, D_MODEL),
)


def _weight_specs(layer, bias_rows):
    specs = []
    for shape in _WEIGHT_SHAPES:
        specs.append(_layer_spec(shape if shape is not None else (bias_rows, LANES))(layer))
    specs.append(pl.BlockSpec((1, D_MODEL), lambda i: (0, 0), pipeline_mode=pl.Buffered(1)))
    return specs


def _prompt_layer(x, weights, layer, final):
    seq = x.shape[0]
    tm = PROMPT_TILE
    assert seq % tm == 0
    out_shape = (
        jax.ShapeDtypeStruct((seq, D_MODEL), F32),
        jax.ShapeDtypeStruct((H_B, DK, DK), F32),
        jax.ShapeDtypeStruct((SUBLANES, D_QKV), F32),
    )
    return pl.pallas_call(
        functools.partial(_prompt_kernel, final=final),
        grid=(seq // tm,),
        in_specs=[pl.BlockSpec((tm, D_MODEL), lambda i: (i, 0))] + _weight_specs(layer, GMLP_BLOCK),
        out_specs=(
            pl.BlockSpec((tm, D_MODEL), lambda i: (i, 0)),
            pl.BlockSpec((H_B, DK, DK), lambda i: (0, 0, 0)),
            pl.BlockSpec((SUBLANES, D_QKV), lambda i: (0, 0)),
        ),
        out_shape=out_shape,
        scratch_shapes=[
            pltpu.VMEM((H_B, DK, DK), F32),
            pltpu.VMEM((SUBLANES, D_QKV), F32),
            pltpu.VMEM((PROMPT_GROUP, D_B), F32),
        ],
        compiler_params=pltpu.CompilerParams(dimension_semantics=("arbitrary",),
                                             vmem_limit_bytes=VMEM_LIMIT),
        name=f"prompt_layer{layer}",
    )(x, *weights)


def _sample_layer(x, s0, cst, weights, layer, final, length):
    nb = SAMPLE_SEQS
    tm = nb * length
    ntile = x.shape[0] // tm
    hist = (CONV_W - 1) * nb
    out_shape = (
        jax.ShapeDtypeStruct((ntile * tm, D_MODEL), F32),
        jax.ShapeDtypeStruct((ntile * nb, H_B, DK, DK), F32),
        jax.ShapeDtypeStruct((ntile, hist, D_QKV), F32),
        jax.ShapeDtypeStruct((ntile * tm, D_A), F32),
    )
    return pl.pallas_call(
        functools.partial(_sample_kernel, final=final, length=length),
        grid=(ntile,),
        in_specs=[
            pl.BlockSpec((tm, D_MODEL), lambda i: (i, 0)),
            pl.BlockSpec((None, nb, H_B, DK, DK), lambda i, _l=layer: (_l, i, 0, 0, 0)),
            pl.BlockSpec((None, None, hist, D_QKV), lambda i, _l=layer: (_l, i, 0, 0)),
        ] + _weight_specs(layer, tm),
        out_specs=(
            pl.BlockSpec((tm, D_MODEL), lambda i: (i, 0)),
            pl.BlockSpec((nb, H_B, DK, DK), lambda i: (i, 0, 0, 0)),
            pl.BlockSpec((None, hist, D_QKV), lambda i: (i, 0, 0)),
            pl.BlockSpec((tm, D_A), lambda i: (i, 0)),
        ),
        out_shape=out_shape,
        compiler_params=pltpu.CompilerParams(dimension_semantics=("arbitrary",),
                                             vmem_limit_bytes=VMEM_LIMIT),
        name=f"sample_layer{layer}",
    )(x, s0, cst, *weights)


def kernel(x_prompt, x_sample, state_gdn, state_conv, w_in, conv_w, a_log, dt_bias, gdn_norm, gmlp_ws, gmlp_bs, gmlp_norm, w_o, norm1, norm2, w_gate, w_up, w_down, final_norm):
    depth = w_in.shape[0]
    bp, seq, _ = x_prompt.shape
    nseq, length, _ = x_sample.shape
    assert bp == 1 and nseq % SAMPLE_SEQS == 0 and length % DN_CHUNK != 0
    nb = SAMPLE_SEQS
    ntile = nseq // nb

    w_ba = jnp.pad(w_in[:, :, P_MAIN:], ((0, 0), (0, 0), (0, LANES - 2 * H_B))).astype(BF16)
    lane_pad = ((0, 0), (0, 0), (H_B, LANES - 2 * H_B))
    bs_pad = jnp.pad(jnp.swapaxes(gmlp_bs, 1, 2), ((0, 0), (0, 0), (0, LANES - G_A)))

    def weights(bias):
        return (
            w_in[:, :, :P_MAIN].astype(BF16), w_ba, conv_w,
            jnp.pad(a_log[:, None, :], lane_pad), jnp.pad(dt_bias[:, None, :], lane_pad),
            gdn_norm[:, None, :], gmlp_ws, bias, gmlp_norm[:, None, :],
            w_o.astype(BF16), norm1[:, None, :], norm2[:, None, :],
            w_gate.astype(BF16), w_up.astype(BF16), w_down.astype(BF16), final_norm[None, :],
        )

    w_prompt = weights(bs_pad)
    w_sample = (w_prompt[:7] + (jnp.repeat(bs_pad[:, :length], nb, axis=1),) + w_prompt[8:])

    xs = x_sample.reshape(ntile, nb, length, D_MODEL).swapaxes(1, 2).reshape(nseq * length, D_MODEL)
    cst = (state_conv.reshape(depth, ntile, nb, CONV_W - 1, D_QKV).swapaxes(2, 3)
           .reshape(depth, ntile, (CONV_W - 1) * nb, D_QKV))
    xp = x_prompt.reshape(seq, D_MODEL)

    gdn_p, conv_p, gdn_s, conv_s, v_s = [], [], [], [], []
    for l in range(depth):
        final = l == depth - 1
        xp, s_fin, conv_tail = _prompt_layer(xp, w_prompt, l, final)
        gdn_p.append(s_fin[None])
        conv_p.append(conv_tail[None, SUBLANES - (CONV_W - 1):])
        xs, s_new, conv_new, v_a = _sample_layer(xs, state_gdn, cst, w_sample, l, final, length)
        gdn_s.append(s_new)
        conv_s.append(conv_new)
        v_s.append(v_a)

    def unorder(t, width):
        return (t.reshape(t.shape[:-2] + (ntile, length, nb, width)).swapaxes(-2, -3)
                .reshape(t.shape[:-2] + (nseq, length, width)))

    conv_s = (jnp.stack(conv_s).reshape(depth, ntile, CONV_W - 1, nb, D_QKV).swapaxes(2, 3)
              .reshape(depth, nseq, CONV_W - 1, D_QKV))
    return (xp.reshape(1, seq, D_MODEL), unorder(xs, D_MODEL), jnp.stack(gdn_p), jnp.stack(conv_p),
            jnp.stack(gdn_s), conv_s, unorder(jnp.stack(v_s), D_A))
```

```python
import functools

import jax
import jax.numpy as jnp
from jax import lax
from jax.experimental import pallas as pl
from jax.experimental.pallas import tpu as pltpu

D_MODEL = 1024
D_A = 512
G_A = 4
C_A = D_A // G_A
GMLP_BLOCK = 128
GMLP_CAUSAL = 64
D_B = 512
H_B = 4
DK = D_B // H_B
CONV_W = 4
D_QKV = 3 * D_B
DN_CHUNK = 64
D_FF = 2816
P_MAIN = 2 * D_A + 4 * D_B
EPS = 1e-6

LANES = 128
SUBLANES = 8
PROMPT_TILE = 512
PROMPT_GROUP = 256
FFN_SLAB = 256
SAMPLE_SEQS = 8
VMEM_LIMIT = 58 * 1024 * 1024

BF16 = jnp.bfloat16
F32 = jnp.float32


def _mm(a, b):
    return jnp.dot(a.astype(BF16), b.astype(BF16), preferred_element_type=F32)


def _mm_nt(a, b):
    return lax.dot_general(a.astype(BF16), b.astype(BF16), (((1,), (1,)), ((), ())),
                           preferred_element_type=F32)


def _rms(x, g):
    return x * lax.rsqrt(jnp.mean(x * x, axis=-1, keepdims=True) + EPS) * g


def _softplus(x):
    return jnp.maximum(x, 0.0) + jnp.log1p(jnp.exp(-jnp.abs(x)))


def _iota(shape, axis):
    return lax.broadcasted_iota(jnp.int32, shape, axis)


def _chunk_cumsum(g, stride, length):
    pos = (_iota(g.shape, 0) // stride) % length
    k = 1
    while k < length:
        g = g + jnp.where(pos >= k, pltpu.roll(g, k * stride, 0), 0.0)
        k *= 2
    return g


def _unit_lower_inverse(ms, levels):
    n = ms[0].shape[0]
    eye = (_iota((n, n), 0) == _iota((n, n), 1)).astype(F32)
    rs = [eye - m for m in ms]
    qs = list(ms)
    for _ in range(1, levels):
        qs = [_mm(q, q) for q in qs]
        rs = [r + _mm(r, q) for r, q in zip(rs, qs)]
    return rs


def _in_proj(x, n1_ref, w_in_ref, w_ba_ref):
    hb = _rms(x, n1_ref[...]).astype(BF16)
    p = jnp.dot(hb, w_in_ref[...], preferred_element_type=F32)
    pba = jnp.dot(hb, w_ba_ref[...], preferred_element_type=F32)
    return p, pba


def _group_rms(y, g, width):
    outs = []
    for j in range(y.shape[1] // width):
        sl = slice(j * width, (j + 1) * width)
        outs.append(_rms(y[:, sl], g[:, sl]))
    return jnp.concatenate(outs, axis=1)


def _l2norm_heads(x):
    outs = []
    for h in range(H_B):
        xh = x[:, h * DK:(h + 1) * DK]
        outs.append(xh * lax.rsqrt(jnp.sum(xh * xh, axis=-1, keepdims=True) + EPS))
    return outs


def _gates(pba, alog_ref, dtb_ref):
    beta = jax.nn.sigmoid(pba)
    g = -jnp.exp(alog_ref[...]) * _softplus(pba + dtb_ref[...])
    return beta, g


def _out_ffn(x, y_mix, w_o_ref, n2_ref, wg_ref, wu_ref, wd_ref, fn_ref, final):
    x1 = x + jnp.dot(y_mix.astype(BF16), w_o_ref[...], preferred_element_type=F32)
    h2 = _rms(x1, n2_ref[...]).astype(BF16)
    gate = jnp.dot(h2, wg_ref[...], preferred_element_type=F32)
    up = jnp.dot(h2, wu_ref[...], preferred_element_type=F32)
    act = (jax.nn.silu(gate) * up).astype(BF16)
    x2 = x1 + jnp.dot(act, wd_ref[...], preferred_element_type=F32)
    if final:
        x2 = _rms(x2, fn_ref[...])
    return x2


def _gated_out_norm(o, z, gdnn_ref):
    g = jnp.concatenate([gdnn_ref[...]] * H_B, axis=1)
    return _group_rms(o, g, DK) * jax.nn.silu(z)


def _interleave(*stage_lists):
    lists = [s for s in stage_lists if s]
    done = [0] * len(lists)
    for _ in range(sum(len(s) for s in lists)):
        k = min((i for i in range(len(lists)) if done[i] < len(lists[i])),
                key=lambda i: (done[i] + 1) / len(lists[i]))
        lists[k][done[k]]()
        done[k] += 1


def _proj_stages(ctx, x_ref, rows, n1_ref, w_in_ref, w_ba_ref):
    half = D_B // 2

    def norm():
        ctx["hb"] = _rms(x_ref[rows, :], n1_ref[...]).astype(BF16)

    def slab(name, lo, second):
        def run():
            part = jnp.dot(ctx["hb"], w_in_ref[:, lo:lo + half], preferred_element_type=F32)
            ctx[name] = jnp.concatenate([ctx[name], part], axis=1) if second else part
        return run

    def gates():
        ctx["pba"] = jnp.dot(ctx["hb"], w_ba_ref[...], preferred_element_type=F32)

    names = ("u", "v", "q", "k", "vv", "z")
    slabs = [slab(n, i * D_B + s * half, s == 1) for i, n in enumerate(names) for s in range(2)]
    return [norm] + slabs + [gates]


def _mix_stages(ctx, rows, cw_ref, alog_ref, dtb_ref, gdnn_ref, ws_ref, bst_ref, gmn_ref,
                s_out_ref, conv_out_ref, s_scr, prev_scr, o_scr):
    tm = rows.size
    nchunk = tm // DN_CHUNK
    probs = [(h, pr) for h in range(H_B) for pr in range(tm // LANES)]
    ri = _iota((LANES, LANES), 0)
    ci = _iota((LANES, LANES), 1)

    def lane(h):
        return slice(H_B + h, H_B + h + 1)

    def rows_of(pr):
        return slice(pr * LANES, (pr + 1) * LANES)

    def gmlp():
        u_a = jax.nn.gelu(ctx["u"])
        v_a = jax.nn.gelu(ctx["v"])
        bi = _iota((GMLP_BLOCK, GMLP_BLOCK), 0) // GMLP_CAUSAL
        bj = _iota((GMLP_BLOCK, GMLP_BLOCK), 1) // GMLP_CAUSAL
        bst = bst_ref[...]
        y_cols = []
        for g in range(G_A):
            wm = jnp.where(bj <= bi, ws_ref[g], 0.0)
            blocks = []
            for r in range(tm // GMLP_BLOCK):
                rw = slice(r * GMLP_BLOCK, (r + 1) * GMLP_BLOCK)
                cols = slice(g * C_A, (g + 1) * C_A)
                blocks.append(u_a[rw, cols] * (_mm(wm, v_a[rw, cols]) + bst[:, g:g + 1]))
            y_cols.append(jnp.concatenate(blocks, axis=0))
        ctx["y_a"] = _group_rms(jnp.concatenate(y_cols, axis=1), gmn_ref[...], C_A)

    def conv():
        row8 = _iota((SUBLANES, D_B), 0)
        outs = []
        for i, name in enumerate(("q", "k", "vv")):
            cols = slice(i * D_B, (i + 1) * D_B)
            x = ctx[name]
            prev = prev_scr[:, cols]
            acc = cw_ref[CONV_W - 1:CONV_W, cols] * x
            for s in range(1, CONV_W):
                rolled = pltpu.roll(x, s, 0)
                head = jnp.where(row8 < s, pltpu.roll(prev, s, 0), rolled[0:SUBLANES])
                shifted = jnp.concatenate([head, rolled[SUBLANES:]], axis=0)
                acc = acc + cw_ref[CONV_W - 1 - s:CONV_W - s, cols] * shifted
            prev_scr[:, cols] = x[tm - SUBLANES:tm]
            conv_out_ref[:, cols] = x[tm - SUBLANES:tm]
            outs.append(jax.nn.silu(acc))
        ctx["qn"] = _l2norm_heads(outs[0])
        ctx["kn"] = _l2norm_heads(outs[1])
        ctx["vc"] = outs[2]

    def gates():
        beta, g = _gates(ctx["pba"], alog_ref, dtb_ref)
        bc = _chunk_cumsum(g, 1, DN_CHUNK)
        blast = jnp.concatenate(
            [jnp.broadcast_to(bc[(c + 1) * DN_CHUNK - 1:(c + 1) * DN_CHUNK, :], (DN_CHUNK, LANES))
             for c in range(nchunk)], axis=0)
        ctx.update(beta=beta, bc=bc, bc_t=bc.T, blast_t=blast.T)

    def chunk_mats():
        same = (ri // DN_CHUNK) == (ci // DN_CHUNK)
        incl = same & (ri >= ci)
        strict = same & (ri > ci)
        bc, bc_t, beta, qn, kn = ctx["bc"], ctx["bc_t"], ctx["beta"], ctx["qn"], ctx["kn"]
        ms, attn = [], []
        for h, pr in probs:
            rw = rows_of(pr)
            dec = jnp.exp(jnp.where(incl, bc[rw, lane(h)] - bc_t[lane(h), rw], -jnp.inf))
            kq = _mm_nt(jnp.concatenate([kn[h][rw], qn[h][rw] * (DK ** -0.5)], axis=0), kn[h][rw])
            ms.append(beta[rw, h:h + 1] * kq[0:LANES] * jnp.where(strict, dec, 0.0))
            attn.append(kq[LANES:2 * LANES] * dec)
        eye = (ri == ci).astype(F32)
        ctx.update(attn=attn, qs=[-m for m in ms], rs=[eye - m for m in ms])

    def inverse_level(last):
        def run():
            rs, qs = [], []
            for r, q in zip(ctx["rs"], ctx["qs"]):
                if last:
                    rs.append(r + _mm(r, q))
                else:
                    rq = _mm(jnp.concatenate([r, q], axis=0), q)
                    rs.append(r + rq[0:LANES])
                    qs.append(rq[LANES:2 * LANES])
            ctx.update(rs=rs, qs=qs)
        return run

    def first_level():
        ctx["qs"] = [_mm(q, q) for q in ctx["qs"]]

    def solve():
        bc, bc_t, beta, qn, kn = ctx["bc"], ctx["bc_t"], ctx["beta"], ctx["qn"], ctx["kn"]
        uw, qd, kd_t = [], [], []
        for (h, pr), t_i in zip(probs, ctx["rs"]):
            rw = rows_of(pr)
            beta_c = beta[rw, h:h + 1]
            ebc = jnp.exp(bc[rw, lane(h)])
            rhs = jnp.concatenate([kn[h][rw] * (beta_c * ebc), ctx["vc"][rw, h * DK:(h + 1) * DK] * beta_c],
                                  axis=1)
            uw.append(_mm(t_i, rhs))
            qd.append(qn[h][rw] * (DK ** -0.5) * ebc)
            kd_t.append(kn[h][rw].T * jnp.exp(ctx["blast_t"][lane(h), rw] - bc_t[lane(h), rw]))
        ctx.update(uw=uw, qd=qd, kd_t=kd_t)

    def chunk_terms():
        col_half = ci // DN_CHUNK
        terms = {}
        for i, (h, pr) in enumerate(probs):
            kd_t = ctx["kd_t"][i]
            lhs = jnp.concatenate([jnp.where(col_half == 0, kd_t, 0.0), jnp.where(col_half == 1, kd_t, 0.0),
                                   ctx["attn"][i]], axis=0)
            prod = _mm(lhs, ctx["uw"][i])
            for half in range(2):
                kw = prod[half * DK:(half + 1) * DK]
                aw = prod[2 * DK + half * DN_CHUNK:2 * DK + (half + 1) * DN_CHUNK]
                q_eff = ctx["qd"][i][half * DN_CHUNK:(half + 1) * DN_CHUNK] - aw[:, 0:DK]
                terms[h, 2 * pr + half] = (jnp.concatenate([q_eff, -kw[:, 0:DK]], axis=0).astype(BF16),
                                           aw[:, DK:2 * DK], kw[:, DK:2 * DK])
        ctx["terms"] = terms
        ctx["s"] = [s_scr[h] for h in range(H_B)]

    def recur(c):
        def run():
            last = (c + 1) * DN_CHUNK - 1
            for h in range(H_B):
                lhs, o_add, b_add = ctx["terms"][h, c]
                prod = _mm(lhs, ctx["s"][h])
                o_scr[c * DN_CHUNK:(c + 1) * DN_CHUNK, h * DK:(h + 1) * DK] = prod[0:DN_CHUNK] + o_add
                g_last = jnp.exp(ctx["bc"][last:last + 1, lane(h)])
                ctx["s"][h] = g_last * ctx["s"][h] + prod[DN_CHUNK:DN_CHUNK + DK] + b_add
        return run

    def finish():
        for h in range(H_B):
            s_scr[h] = ctx["s"][h]
            s_out_ref[h] = ctx["s"][h]
        y_b = _gated_out_norm(o_scr[0:tm, :], ctx["z"], gdnn_ref)
        ctx["y_mix"] = jnp.concatenate([ctx["y_a"], y_b], axis=1).astype(BF16)

    return ([gmlp, conv, gates, chunk_mats, first_level] + [inverse_level(False)] * 4
            + [inverse_level(True), solve, chunk_terms]
            + [recur(c) for c in range(nchunk)] + [finish])


def _ffn_stages(ctx, x_ref, y_ref, rows, w_o_ref, n2_ref, wg_ref, wu_ref, wd_ref, fn_ref, final):
    def out_proj():
        x1 = x_ref[rows, :] + jnp.dot(ctx["y_mix"], w_o_ref[...], preferred_element_type=F32)
        ctx["h2"] = _rms(x1, n2_ref[...]).astype(BF16)
        ctx["acc"] = x1

    def gate_up(lo):
        gate = jnp.dot(ctx["h2"], wg_ref[:, lo:lo + FFN_SLAB], preferred_element_type=F32)
        up = jnp.dot(ctx["h2"], wu_ref[:, lo:lo + FFN_SLAB], preferred_element_type=F32)
        ctx["act", lo] = (jax.nn.silu(gate) * up).astype(BF16)

    def down(lo):
        ctx["acc"] = ctx["acc"] + jnp.dot(ctx.pop(("act", lo)), wd_ref[lo:lo + FFN_SLAB, :],
                                          preferred_element_type=F32)

    def slab(lo):
        def run():
            if lo < D_FF:
                gate_up(lo)
            if lo > 0:
                down(lo - FFN_SLAB)
        return run

    def store():
        x2 = ctx["acc"]
        if final:
            x2 = _rms(x2, fn_ref[...])
        y_ref[rows, :] = x2

    assert D_FF % FFN_SLAB == 0
    return [out_proj] + [slab(lo) for lo in range(0, D_FF + FFN_SLAB, FFN_SLAB)] + [store]


def _prompt_kernel(x_ref, w_in_ref, w_ba_ref, cw_ref, alog_ref, dtb_ref, gdnn_ref, ws_ref, bst_ref,
                   gmn_ref, w_o_ref, n1_ref, n2_ref, wg_ref, wu_ref, wd_ref, fn_ref,
                   y_ref, s_out_ref, conv_out_ref, s_scr, prev_scr, o_scr, *, final):
    step = pl.program_id(0)

    @pl.when(step == 0)
    def _():
        s_scr[...] = jnp.zeros_like(s_scr)
        prev_scr[...] = jnp.zeros_like(prev_scr)

    ngroup = PROMPT_TILE // PROMPT_GROUP
    proj, mix, ffn = [], [], []
    for j in range(ngroup):
        ctx = {}
        rows = pl.ds(j * PROMPT_GROUP, PROMPT_GROUP)
        proj.append(_proj_stages(ctx, x_ref, rows, n1_ref, w_in_ref, w_ba_ref))
        mix.append(_mix_stages(ctx, rows, cw_ref, alog_ref, dtb_ref, gdnn_ref, ws_ref, bst_ref, gmn_ref,
                               s_out_ref, conv_out_ref, s_scr, prev_scr, o_scr))
        ffn.append(_ffn_stages(ctx, x_ref, y_ref, rows, w_o_ref, n2_ref, wg_ref, wu_ref, wd_ref, fn_ref,
                               final))
    _interleave(proj[0])
    for j in range(ngroup):
        _interleave(mix[j], proj[j + 1] if j + 1 < ngroup else None, ffn[j - 1] if j > 0 else None)
    _interleave(ffn[ngroup - 1])


def _sample_kernel(x_ref, s0_ref, cst_ref, w_in_ref, w_ba_ref, cw_ref, alog_ref, dtb_ref, gdnn_ref,
                   ws_ref, bst_ref, gmn_ref, w_o_ref, n1_ref, n2_ref, wg_ref, wu_ref, wd_ref, fn_ref,
                   y_ref, s_out_ref, conv_out_ref, va_ref, *, final, length):
    nb = SAMPLE_SEQS
    tm = nb * length
    x = x_ref[...]
    p, pba = _in_proj(x, n1_ref, w_in_ref, w_ba_ref)

    ri = _iota((tm, tm), 0)
    ci = _iota((tm, tm), 1)
    same = (ri % nb) == (ci % nb)

    u_a = jax.nn.gelu(p[:, 0:D_A])
    v_a = jax.nn.gelu(p[:, D_A:2 * D_A])
    va_ref[...] = v_a
    expand = ((_iota((tm, length), 0) // nb) == _iota((tm, length), 1)).astype(F32)
    bst = bst_ref[...]
    y_cols = []
    for g in range(G_A):
        w_small = ws_ref[g][0:length, 0:length]
        wk = jnp.where(same, _mm_nt(_mm(expand, w_small), expand), 0.0)
        cols = slice(g * C_A, (g + 1) * C_A)
        s = _mm(wk, v_a[:, cols]) + bst[:, g:g + 1]
        y_cols.append(u_a[:, cols] * s)
    y_a = _group_rms(jnp.concatenate(y_cols, axis=1), gmn_ref[...], C_A)

    qkv = p[:, 2 * D_A:2 * D_A + D_QKV]
    hist = (CONV_W - 1) * nb
    xp = jnp.concatenate([cst_ref[...], qkv], axis=0)
    conv = cw_ref[0:1, :] * xp[0:tm]
    for i in range(1, CONV_W):
        conv = conv + cw_ref[i:i + 1, :] * xp[i * nb:i * nb + tm]
    conv_out_ref[...] = xp[tm:tm + hist]
    qkv_c = jax.nn.silu(conv)
    qn = _l2norm_heads(qkv_c[:, 0:D_B])
    kn = _l2norm_heads(qkv_c[:, D_B:2 * D_B])
    v = qkv_c[:, 2 * D_B:3 * D_B]
    z = p[:, 2 * D_A + D_QKV:P_MAIN]

    beta, g = _gates(pba, alog_ref, dtb_ref)
    bc = _chunk_cumsum(g, nb, length)
    blast = jnp.concatenate([bc[tm - nb:tm]] * length, axis=0)
    bc_t = bc.T
    blast_t = blast.T
    incl = same & (ri >= ci)
    strict = same & (ri > ci)
    row_seq = _iota((tm, DK), 0) % nb
    row_seq2 = _iota((2 * tm, DK), 0) % nb
    levels = max(1, (length - 1).bit_length())

    o_heads = []
    for h in range(H_B):
        beta_c = beta[:, h:h + 1]
        bc_c = bc[:, H_B + h:H_B + h + 1]
        bc_r = bc_t[H_B + h:H_B + h + 1, :]
        dec = jnp.exp(jnp.where(incl, bc_c - bc_r, -jnp.inf))
        kk = _mm_nt(kn[h], kn[h])
        qk = _mm_nt(qn[h] * (DK ** -0.5), kn[h])
        m = beta_c * kk * jnp.where(strict, dec, 0.0)
        attn = qk * dec
        t_inv = _unit_lower_inverse([m], levels)[0]
        ebc = jnp.exp(bc_c)
        rhs = jnp.concatenate([v[:, h * DK:(h + 1) * DK] * beta_c, kn[h] * (beta_c * ebc)], axis=1)
        uw = _mm(t_inv, rhs)
        u = uw[:, 0:DK]
        w = uw[:, DK:2 * DK]
        qd = qn[h] * (DK ** -0.5) * ebc
        kd_t = kn[h].T * jnp.exp(blast_t[H_B + h:H_B + h + 1, :] - bc_r)

        wq = jnp.concatenate([w, qd], axis=0)
        acc = jnp.zeros((2 * tm, DK), F32)
        for b in range(nb):
            acc = acc + jnp.where(row_seq2 == b, _mm(wq, s0_ref[b, h]), 0.0)
        delta = u - acc[0:tm]
        o_heads.append(acc[tm:2 * tm] + _mm(attn, delta))
        for b in range(nb):
            g_last = jnp.exp(bc[tm - nb + b:tm - nb + b + 1, H_B + h:H_B + h + 1])
            s_out_ref[b, h] = g_last * s0_ref[b, h] + _mm(kd_t, jnp.where(row_seq == b, delta, 0.0))

    y_b = _gated_out_norm(jnp.concatenate(o_heads, axis=1), z, gdnn_ref)
    y_mix = jnp.concatenate([y_a, y_b], axis=1)
    y_ref[...] = _out_ffn(x, y_mix, w_o_ref, n2_ref, wg_ref, wu_ref, wd_ref, fn_ref, final)


def _layer_spec(shape):
    nd = len(shape)

    def make(layer):
        return pl.BlockSpec((None,) + tuple(shape), lambda i, _l=layer: (_l,) + (0,) * nd,
                            pipeline_mode=pl.Buffered(1))
    return make


_WEIGHT_SHAPES = (
    (D_MODEL, P_MAIN),
    (D_MODEL, LANES),
    (CONV_W, D_QKV),
    (1, LANES),
    (1, LANES),
    (1, DK),
    (G_A, GMLP_BLOCK, GMLP_BLOCK),
    None,
    (1, D_A),
    (D_MODEL, D_MODEL),
    (1, D_MODEL),
    (1, D_MODEL),
    (D_MODEL, D_FF),
    (D_MODEL, D_FF),
    (D_FF, D_MODEL),
)


def _weight_specs(layer, bias_rows):
    specs = []
    for shape in _WEIGHT_SHAPES:
        specs.append(_layer_spec(shape if shape is not None else (bias_rows, LANES))(layer))
    specs.append(pl.BlockSpec((1, D_MODEL), lambda i: (0, 0), pipeline_mode=pl.Buffered(1)))
    return specs


def _prompt_layer(x, weights, layer, final):
    seq = x.shape[0]
    tm = PROMPT_TILE
    assert seq % tm == 0
    out_shape = (
        jax.ShapeDtypeStruct((seq, D_MODEL), F32),
        jax.ShapeDtypeStruct((H_B, DK, DK), F32),
        jax.ShapeDtypeStruct((SUBLANES, D_QKV), F32),
    )
    return pl.pallas_call(
        functools.partial(_prompt_kernel, final=final),
        grid=(seq // tm,),
        in_specs=[pl.BlockSpec((tm, D_MODEL), lambda i: (i, 0))] + _weight_specs(layer, GMLP_BLOCK),
        out_specs=(
            pl.BlockSpec((tm, D_MODEL), lambda i: (i, 0)),
            pl.BlockSpec((H_B, DK, DK), lambda i: (0, 0, 0)),
            pl.BlockSpec((SUBLANES, D_QKV), lambda i: (0, 0)),
        ),
        out_shape=out_shape,
        scratch_shapes=[
            pltpu.VMEM((H_B, DK, DK), F32),
            pltpu.VMEM((SUBLANES, D_QKV), F32),
            pltpu.VMEM((PROMPT_GROUP, D_B), F32),
        ],
        compiler_params=pltpu.CompilerParams(dimension_semantics=("arbitrary",),
                                             vmem_limit_bytes=VMEM_LIMIT),
        name=f"prompt_layer{layer}",
    )(x, *weights)


def _sample_layer(x, s0, cst, weights, layer, final, length):
    nb = SAMPLE_SEQS
    tm = nb * length
    ntile = x.shape[0] // tm
    hist = (CONV_W - 1) * nb
    out_shape = (
        jax.ShapeDtypeStruct((ntile * tm, D_MODEL), F32),
        jax.ShapeDtypeStruct((ntile * nb, H_B, DK, DK), F32),
        jax.ShapeDtypeStruct((ntile, hist, D_QKV), F32),
        jax.ShapeDtypeStruct((ntile * tm, D_A), F32),
    )
    return pl.pallas_call(
        functools.partial(_sample_kernel, final=final, length=length),
        grid=(ntile,),
        in_specs=[
            pl.BlockSpec((tm, D_MODEL), lambda i: (i, 0)),
            pl.BlockSpec((None, nb, H_B, DK, DK), lambda i, _l=layer: (_l, i, 0, 0, 0)),
            pl.BlockSpec((None, None, hist, D_QKV), lambda i, _l=layer: (_l, i, 0, 0)),
        ] + _weight_specs(layer, tm),
        out_specs=(
            pl.BlockSpec((tm, D_MODEL), lambda i: (i, 0)),
            pl.BlockSpec((nb, H_B, DK, DK), lambda i: (i, 0, 0, 0)),
            pl.BlockSpec((None, hist, D_QKV), lambda i: (i, 0, 0)),
            pl.BlockSpec((tm, D_A), lambda i: (i, 0)),
        ),
        out_shape=out_shape,
        compiler_params=pltpu.CompilerParams(dimension_semantics=("arbitrary",),
                                             vmem_limit_bytes=VMEM_LIMIT),
        name=f"sample_layer{layer}",
    )(x, s0, cst, *weights)


def kernel(x_prompt, x_sample, state_gdn, state_conv, w_in, conv_w, a_log, dt_bias, gdn_norm, gmlp_ws, gmlp_bs, gmlp_norm, w_o, norm1, norm2, w_gate, w_up, w_down, final_norm):
    depth = w_in.shape[0]
    bp, seq, _ = x_prompt.shape
    nseq, length, _ = x_sample.shape
    assert bp == 1 and nseq % SAMPLE_SEQS == 0 and length % DN_CHUNK != 0
    nb = SAMPLE_SEQS
    ntile = nseq // nb

    w_ba = jnp.pad(w_in[:, :, P_MAIN:], ((0, 0), (0, 0), (0, LANES - 2 * H_B))).astype(BF16)
    lane_pad = ((0, 0), (0, 0), (H_B, LANES - 2 * H_B))
    bs_pad = jnp.pad(jnp.swapaxes(gmlp_bs, 1, 2), ((0, 0), (0, 0), (0, LANES - G_A)))

    def weights(bias):
        return (
            w_in[:, :, :P_MAIN].astype(BF16), w_ba, conv_w,
            jnp.pad(a_log[:, None, :], lane_pad), jnp.pad(dt_bias[:, None, :], lane_pad),
            gdn_norm[:, None, :], gmlp_ws, bias, gmlp_norm[:, None, :],
            w_o.astype(BF16), norm1[:, None, :], norm2[:, None, :],
            w_gate.astype(BF16), w_up.astype(BF16), w_down.astype(BF16), final_norm[None, :],
        )

    w_prompt = weights(bs_pad)
    w_sample = (w_prompt[:7] + (jnp.repeat(bs_pad[:, :length], nb, axis=1),) + w_prompt[8:])

    xs = x_sample.reshape(ntile, nb, length, D_MODEL).swapaxes(1, 2).reshape(nseq * length, D_MODEL)
    cst = (state_conv.reshape(depth, ntile, nb, CONV_W - 1, D_QKV).swapaxes(2, 3)
           .reshape(depth, ntile, (CONV_W - 1) * nb, D_QKV))
    xp = x_prompt.reshape(seq, D_MODEL)

    gdn_p, conv_p, gdn_s, conv_s, v_s = [], [], [], [], []
    for l in range(depth):
        final = l == depth - 1
        xp, s_fin, conv_tail = _prompt_layer(xp, w_prompt, l, final)
        gdn_p.append(s_fin[None])
        conv_p.append(conv_tail[None, SUBLANES - (CONV_W - 1):])
        xs, s_new, conv_new, v_a = _sample_layer(xs, state_gdn, cst, w_sample, l, final, length)
        gdn_s.append(s_new)
        conv_s.append(conv_new)
        v_s.append(v_a)

    def unorder(t, width):
        return (t.reshape(t.shape[:-2] + (ntile, length, nb, width)).swapaxes(-2, -3)
                .reshape(t.shape[:-2] + (nseq, length, width)))

    conv_s = (jnp.stack(conv_s).reshape(depth, ntile, CONV_W - 1, nb, D_QKV).swapaxes(2, 3)
              .reshape(depth, nseq, CONV_W - 1, D_QKV))
    return (xp.reshape(1, seq, D_MODEL), unorder(xs, D_MODEL), jnp.stack(gdn_p), jnp.stack(conv_p),
            jnp.stack(gdn_s), conv_s, unorder(jnp.stack(v_s), D_A))
```

```python
import functools

import jax
import jax.numpy as jnp
from jax import lax
from jax.experimental import pallas as pl
from jax.experimental.pallas import tpu as pltpu

D_MODEL = 1024
D_A = 512
G_A = 4
C_A = D_A // G_A
GMLP_BLOCK = 128
GMLP_CAUSAL = 64
D_B = 512
H_B = 4
DK = D_B // H_B
CONV_W = 4
D_QKV = 3 * D_B
DN_CHUNK = 64
D_FF = 2816
P_MAIN = 2 * D_A + 4 * D_B
EPS = 1e-6

LANES = 128
SUBLANES = 8
PROMPT_TILE = 512
PROMPT_GROUP = 256
FFN_SLAB = 256
SAMPLE_SEQS = 8
VMEM_LIMIT = 58 * 1024 * 1024

BF16 = jnp.bfloat16
F32 = jnp.float32


def _mm(a, b):
    return jnp.dot(a.astype(BF16), b.astype(BF16), preferred_element_type=F32)


def _mm_nt(a, b):
    return lax.dot_general(a.astype(BF16), b.astype(BF16), (((1,), (1,)), ((), ())),
                           preferred_element_type=F32)


def _rms(x, g):
    return x * lax.rsqrt(jnp.mean(x * x, axis=-1, keepdims=True) + EPS) * g


_GELU_C0 = 0.7978845608028654
_GELU_C1 = _GELU_C0 * 0.044715


def _gelu(x):
    hx = 0.5 * x
    return hx + hx * jnp.tanh(x * (_GELU_C0 + _GELU_C1 * (x * x)))


def _softplus(x):
    return jnp.maximum(x, 0.0) + jnp.log1p(jnp.exp(-jnp.abs(x)))


def _iota(shape, axis):
    return lax.broadcasted_iota(jnp.int32, shape, axis)


def _chunk_cumsum(g, stride, length):
    pos = (_iota(g.shape, 0) // stride) % length
    k = 1
    while k < length:
        g = g + jnp.where(pos >= k, pltpu.roll(g, k * stride, 0), 0.0)
        k *= 2
    return g


def _unit_lower_inverse(ms, levels):
    n = ms[0].shape[0]
    eye = (_iota((n, n), 0) == _iota((n, n), 1)).astype(F32)
    rs = [eye - m for m in ms]
    qs = list(ms)
    for _ in range(1, levels):
        qs = [_mm(q, q) for q in qs]
        rs = [r + _mm(r, q) for r, q in zip(rs, qs)]
    return rs


def _in_proj(x, n1_ref, w_in_ref, w_ba_ref):
    hb = _rms(x, n1_ref[...]).astype(BF16)
    p = jnp.dot(hb, w_in_ref[...], preferred_element_type=F32)
    pba = jnp.dot(hb, w_ba_ref[...], preferred_element_type=F32)
    return p, pba


def _group_rms(y, g, width):
    outs = []
    for j in range(y.shape[1] // width):
        sl = slice(j * width, (j + 1) * width)
        outs.append(_rms(y[:, sl], g[:, sl]))
    return jnp.concatenate(outs, axis=1)


def _l2norm_heads(x):
    outs = []
    for h in range(H_B):
        xh = x[:, h * DK:(h + 1) * DK]
        outs.append(xh * lax.rsqrt(jnp.sum(xh * xh, axis=-1, keepdims=True) + EPS))
    return outs


def _gates(pba, alog_ref, dtb_ref):
    beta = jax.nn.sigmoid(pba)
    g = -jnp.exp(alog_ref[...]) * _softplus(pba + dtb_ref[...])
    return beta, g


def _out_ffn(x, y_mix, w_o_ref, n2_ref, wg_ref, wu_ref, wd_ref, fn_ref, final):
    x1 = x + jnp.dot(y_mix.astype(BF16), w_o_ref[...], preferred_element_type=F32)
    h2 = _rms(x1, n2_ref[...]).astype(BF16)
    gate = jnp.dot(h2, wg_ref[...], preferred_element_type=F32)
    up = jnp.dot(h2, wu_ref[...], preferred_element_type=F32)
    act = (jax.nn.silu(gate) * up).astype(BF16)
    x2 = x1 + jnp.dot(act, wd_ref[...], preferred_element_type=F32)
    if final:
        x2 = _rms(x2, fn_ref[...])
    return x2


def _gated_out_norm(o, z, gdnn_ref):
    g = jnp.concatenate([gdnn_ref[...]] * H_B, axis=1)
    return _group_rms(o, g, DK) * jax.nn.silu(z)


def _interleave(*stage_lists):
    lists = [[s if isinstance(s, tuple) else (s, 1) for s in stages] for stages in stage_lists if stages]
    totals = [sum(w for _, w in stages) for stages in lists]
    done = [0] * len(lists)
    spent = [0] * len(lists)
    for _ in range(sum(len(s) for s in lists)):
        k = min((i for i in range(len(lists)) if done[i] < len(lists[i])),
                key=lambda i: (spent[i] + lists[i][done[i]][1]) / totals[i])
        fn, weight = lists[k][done[k]]
        fn()
        done[k] += 1
        spent[k] += weight


def _proj_stages(ctx, x_ref, rows, n1_ref, w_in_ref, w_ba_ref):
    half = D_B // 2

    def norm():
        ctx["hb"] = _rms(x_ref[rows, :], n1_ref[...]).astype(BF16)

    def slab(name, lo, second):
        def run():
            part = jnp.dot(ctx["hb"], w_in_ref[:, lo:lo + half], preferred_element_type=F32)
            if second:
                ctx[name] = jnp.concatenate([ctx.pop(name + "/lo"), part], axis=1)
            else:
                ctx[name + "/lo"] = part
        return run

    def gates():
        ctx["pba"] = jnp.dot(ctx["hb"], w_ba_ref[...], preferred_element_type=F32)

    names = ("u", "v", "q", "k", "vv", "z")
    slabs = {n: [slab(n, i * D_B + s * half, s == 1) for s in range(2)] for i, n in enumerate(names)}
    return [norm], slabs["q"] + slabs["k"] + slabs["vv"] + [gates], slabs["u"] + slabs["v"] + slabs["z"]


def _mix_stages(ctx, rows, cw_ref, alog_ref, dtb_ref, gdnn_ref, ws_ref, bst_ref, gmn_ref,
                s_out_ref, conv_out_ref, s_scr, prev_scr, o_scr):
    tm = rows.size
    nchunk = tm // DN_CHUNK
    probs = [(h, pr) for h in range(H_B) for pr in range(tm // LANES)]
    ri = _iota((LANES, LANES), 0)
    ci = _iota((LANES, LANES), 1)

    def lane(h):
        return slice(H_B + h, H_B + h + 1)

    def rows_of(pr):
        return slice(pr * LANES, (pr + 1) * LANES)

    def gmlp_act():
        ctx["u_a"] = _gelu(ctx["u"])
        ctx["v_a"] = _gelu(ctx["v"])

    def gmlp_gate():
        bi = _iota((GMLP_BLOCK, GMLP_BLOCK), 0) // GMLP_CAUSAL
        bj = _iota((GMLP_BLOCK, GMLP_BLOCK), 1) // GMLP_CAUSAL
        bst = bst_ref[...]
        y_cols = []
        for g in range(G_A):
            wm = jnp.where(bj <= bi, ws_ref[g], 0.0)
            blocks = []
            for r in range(tm // GMLP_BLOCK):
                rw = slice(r * GMLP_BLOCK, (r + 1) * GMLP_BLOCK)
                cols = slice(g * C_A, (g + 1) * C_A)
                blocks.append(ctx["u_a"][rw, cols] * (_mm(wm, ctx["v_a"][rw, cols]) + bst[:, g:g + 1]))
            y_cols.append(jnp.concatenate(blocks, axis=0))
        ctx["y_gate"] = jnp.concatenate(y_cols, axis=1)

    def gmlp_norm():
        ctx["y_a"] = _group_rms(ctx["y_gate"], gmn_ref[...], C_A)

    def conv():
        nblk = tm // SUBLANES
        row = _iota((nblk, SUBLANES, D_B), 1)
        outs = []
        for i, name in enumerate(("q", "k", "vv")):
            cols = slice(i * D_B, (i + 1) * D_B)
            x = ctx[name]
            x3 = x.reshape(nblk, SUBLANES, D_B)
            prev = prev_scr[:, cols]
            acc = cw_ref[CONV_W - 1:CONV_W, cols] * x
            for s in range(1, CONV_W):
                rot = pltpu.roll(x3, s, 1)
                above = jnp.concatenate([pltpu.roll(prev, s, 0)[None], rot[:nblk - 1]], axis=0)
                shifted = jnp.where(row < s, above, rot).reshape(tm, D_B)
                acc = acc + cw_ref[CONV_W - 1 - s:CONV_W - s, cols] * shifted
            prev_scr[:, cols] = x[tm - SUBLANES:tm]
            conv_out_ref[:, cols] = x[tm - SUBLANES:tm]
            outs.append(jax.nn.silu(acc))
        ctx["qn"] = _l2norm_heads(outs[0])
        ctx["kn"] = _l2norm_heads(outs[1])
        ctx["vc"] = outs[2]

    def gates():
        beta, g = _gates(ctx["pba"], alog_ref, dtb_ref)
        bc = _chunk_cumsum(g, 1, DN_CHUNK)
        blast = jnp.concatenate(
            [jnp.broadcast_to(bc[(c + 1) * DN_CHUNK - 1:(c + 1) * DN_CHUNK, :], (DN_CHUNK, LANES))
             for c in range(nchunk)], axis=0)
        ctx.update(beta=beta, bc=bc, bc_t=bc.T, blast_t=blast.T)

    def chunk_mats():
        same = (ri // DN_CHUNK) == (ci // DN_CHUNK)
        incl = same & (ri >= ci)
        strict = same & (ri > ci)
        bc, bc_t, beta, qn, kn = ctx["bc"], ctx["bc_t"], ctx["beta"], ctx["qn"], ctx["kn"]
        ms, attn = [], []
        for h, pr in probs:
            rw = rows_of(pr)
            dec = jnp.exp(jnp.where(incl, bc[rw, lane(h)] - bc_t[lane(h), rw], -jnp.inf))
            kq = _mm_nt(jnp.concatenate([kn[h][rw], qn[h][rw] * (DK ** -0.5)], axis=0), kn[h][rw])
            ms.append(beta[rw, h:h + 1] * kq[0:LANES] * jnp.where(strict, dec, 0.0))
            attn.append(kq[LANES:2 * LANES] * dec)
        eye = (ri == ci).astype(F32)
        ctx.update(attn=attn, qs=[-m for m in ms], rs=[eye - m for m in ms])

    def inverse_level(last):
        def run():
            rs, qs = [], []
            for r, q in zip(ctx["rs"], ctx["qs"]):
                if last:
                    rs.append(r + _mm(r, q))
                else:
                    rq = _mm(jnp.concatenate([r, q], axis=0), q)
                    rs.append(r + rq[0:LANES])
                    qs.append(rq[LANES:2 * LANES])
            ctx.update(rs=rs, qs=qs)
        return run

    def first_level():
        ctx["qs"] = [_mm(q, q) for q in ctx["qs"]]

    def solve():
        bc, bc_t, beta, qn, kn = ctx["bc"], ctx["bc_t"], ctx["beta"], ctx["qn"], ctx["kn"]
        uw, qd, kd_t = [], [], []
        for (h, pr), t_i in zip(probs, ctx["rs"]):
            rw = rows_of(pr)
            beta_c = beta[rw, h:h + 1]
            ebc = jnp.exp(bc[rw, lane(h)])
            rhs = jnp.concatenate([kn[h][rw] * (beta_c * ebc), ctx["vc"][rw, h * DK:(h + 1) * DK] * beta_c],
                                  axis=1)
            uw.append(_mm(t_i, rhs))
            qd.append(qn[h][rw] * (DK ** -0.5) * ebc)
            kd_t.append(kn[h][rw].T * jnp.exp(ctx["blast_t"][lane(h), rw] - bc_t[lane(h), rw]))
        ctx.update(uw=uw, qd=qd, kd_t=kd_t)

    def chunk_terms():
        col_half = ci // DN_CHUNK
        terms = {}
        for i, (h, pr) in enumerate(probs):
            kd_t = ctx["kd_t"][i]
            lhs = jnp.concatenate([jnp.where(col_half == 0, kd_t, 0.0), jnp.where(col_half == 1, kd_t, 0.0),
                                   ctx["attn"][i]], axis=0)
            prod = _mm(lhs, ctx["uw"][i])
            for half in range(2):
                kw = prod[half * DK:(half + 1) * DK]
                aw = prod[2 * DK + half * DN_CHUNK:2 * DK + (half + 1) * DN_CHUNK]
                q_eff = ctx["qd"][i][half * DN_CHUNK:(half + 1) * DN_CHUNK] - aw[:, 0:DK]
                terms[h, 2 * pr + half] = (jnp.concatenate([q_eff, -kw[:, 0:DK]], axis=0).astype(BF16),
                                           aw[:, DK:2 * DK], kw[:, DK:2 * DK])
        ctx["terms"] = terms
        ctx["s"] = [s_scr[h] for h in range(H_B)]

    def recur(c):
        def run():
            last = (c + 1) * DN_CHUNK - 1
            for h in range(H_B):
                lhs, o_add, b_add = ctx["terms"][h, c]
                prod = _mm(lhs, ctx["s"][h])
                o_scr[c * DN_CHUNK:(c + 1) * DN_CHUNK, h * DK:(h + 1) * DK] = prod[0:DN_CHUNK] + o_add
                g_last = jnp.exp(ctx["bc"][last:last + 1, lane(h)])
                ctx["s"][h] = g_last * ctx["s"][h] + prod[DN_CHUNK:DN_CHUNK + DK] + b_add
        return run

    def finish():
        for h in range(H_B):
            s_scr[h] = ctx["s"][h]
            s_out_ref[h] = ctx["s"][h]
        y_b = _gated_out_norm(o_scr[0:tm, :], ctx["z"], gdnn_ref)
        ctx["y_mix"] = jnp.concatenate([ctx["y_a"], y_b], axis=1).astype(BF16)

    fillers = [gmlp_act, gmlp_gate, gmlp_norm]
    tail = []
    for c in range(nchunk):
        tail.append(recur(c))
        if c < len(fillers):
            tail.append(fillers[c])
    assert nchunk >= len(fillers)
    return ([(conv, 8), gates, (chunk_mats, 2), first_level] + [inverse_level(False)] * 4
            + [inverse_level(True), solve, (chunk_terms, 2)] + tail + [finish])


def _ffn_stages(ctx, x_ref, y_ref, rows, w_o_ref, n2_ref, wg_ref, wu_ref, wd_ref, fn_ref, final):
    def out_proj():
        x1 = x_ref[rows, :] + jnp.dot(ctx["y_mix"], w_o_ref[...], preferred_element_type=F32)
        ctx["h2"] = _rms(x1, n2_ref[...]).astype(BF16)
        ctx["acc"] = x1

    def gate_up(lo):
        gate = jnp.dot(ctx["h2"], wg_ref[:, lo:lo + FFN_SLAB], preferred_element_type=F32)
        up = jnp.dot(ctx["h2"], wu_ref[:, lo:lo + FFN_SLAB], preferred_element_type=F32)
        ctx["act", lo] = (jax.nn.silu(gate) * up).astype(BF16)

    def down(lo):
        ctx["acc"] = ctx["acc"] + jnp.dot(ctx.pop(("act", lo)), wd_ref[lo:lo + FFN_SLAB, :],
                                          preferred_element_type=F32)

    def slab(lo):
        def run():
            if lo < D_FF:
                gate_up(lo)
            if lo > 0:
                down(lo - FFN_SLAB)
        return run

    def store():
        x2 = ctx["acc"]
        if final:
            x2 = _rms(x2, fn_ref[...])
        y_ref[rows, :] = x2

    assert D_FF % FFN_SLAB == 0
    return [out_proj] + [slab(lo) for lo in range(0, D_FF + FFN_SLAB, FFN_SLAB)] + [store]


def _prompt_kernel(x_ref, w_in_ref, w_ba_ref, cw_ref, alog_ref, dtb_ref, gdnn_ref, ws_ref, bst_ref,
                   gmn_ref, w_o_ref, n1_ref, n2_ref, wg_ref, wu_ref, wd_ref, fn_ref,
                   y_ref, s_out_ref, conv_out_ref, s_scr, prev_scr, o_scr, *, final):
    step = pl.program_id(0)

    @pl.when(step == 0)
    def _():
        s_scr[...] = jnp.zeros_like(s_scr)
        prev_scr[...] = jnp.zeros_like(prev_scr)

    ngroup = PROMPT_TILE // PROMPT_GROUP
    proj, mix, ffn = [], [], []
    for j in range(ngroup):
        ctx = {}
        rows = pl.ds(j * PROMPT_GROUP, PROMPT_GROUP)
        proj.append(_proj_stages(ctx, x_ref, rows, n1_ref, w_in_ref, w_ba_ref))
        mix.append(_mix_stages(ctx, rows, cw_ref, alog_ref, dtb_ref, gdnn_ref, ws_ref, bst_ref, gmn_ref,
                               s_out_ref, conv_out_ref, s_scr, prev_scr, o_scr))
        ffn.append(_ffn_stages(ctx, x_ref, y_ref, rows, w_o_ref, n2_ref, wg_ref, wu_ref, wd_ref, fn_ref,
                               final))
    for norm, _, _ in proj:
        _interleave(norm)
    _interleave(proj[0][1] + (proj[0][2] if ngroup == 1 else []))
    for j in range(ngroup):
        bulk = (proj[0][2] if j == 0 and ngroup > 1 else []) + (
            proj[j + 1][1] + proj[j + 1][2] if j + 1 < ngroup else [])
        _interleave(mix[j], bulk, ffn[j - 1] if j > 0 else None)
    _interleave(ffn[ngroup - 1])


def _sample_kernel(x_ref, s0_ref, cst_ref, w_in_ref, w_ba_ref, cw_ref, alog_ref, dtb_ref, gdnn_ref,
                   ws_ref, bst_ref, gmn_ref, w_o_ref, n1_ref, n2_ref, wg_ref, wu_ref, wd_ref, fn_ref,
                   y_ref, s_out_ref, conv_out_ref, va_ref, *, final, length):
    nb = SAMPLE_SEQS
    tm = nb * length
    x = x_ref[...]
    p, pba = _in_proj(x, n1_ref, w_in_ref, w_ba_ref)

    ri = _iota((tm, tm), 0)
    ci = _iota((tm, tm), 1)
    same = (ri % nb) == (ci % nb)

    u_a = _gelu(p[:, 0:D_A])
    v_a = _gelu(p[:, D_A:2 * D_A])
    va_ref[...] = v_a
    expand = ((_iota((tm, length), 0) // nb) == _iota((tm, length), 1)).astype(F32)
    bst = bst_ref[...]
    y_cols = []
    for g in range(G_A):
        w_small = ws_ref[g][0:length, 0:length]
        wk = jnp.where(same, _mm_nt(_mm(expand, w_small), expand), 0.0)
        cols = slice(g * C_A, (g + 1) * C_A)
        s = _mm(wk, v_a[:, cols]) + bst[:, g:g + 1]
        y_cols.append(u_a[:, cols] * s)
    y_a = _group_rms(jnp.concatenate(y_cols, axis=1), gmn_ref[...], C_A)

    qkv = p[:, 2 * D_A:2 * D_A + D_QKV]
    hist = (CONV_W - 1) * nb
    xp = jnp.concatenate([cst_ref[...], qkv], axis=0)
    conv = cw_ref[0:1, :] * xp[0:tm]
    for i in range(1, CONV_W):
        conv = conv + cw_ref[i:i + 1, :] * xp[i * nb:i * nb + tm]
    conv_out_ref[...] = xp[tm:tm + hist]
    qkv_c = jax.nn.silu(conv)
    qn = _l2norm_heads(qkv_c[:, 0:D_B])
    kn = _l2norm_heads(qkv_c[:, D_B:2 * D_B])
    v = qkv_c[:, 2 * D_B:3 * D_B]
    z = p[:, 2 * D_A + D_QKV:P_MAIN]

    beta, g = _gates(pba, alog_ref, dtb_ref)
    bc = _chunk_cumsum(g, nb, length)
    blast = jnp.concatenate([bc[tm - nb:tm]] * length, axis=0)
    bc_t = bc.T
    blast_t = blast.T
    incl = same & (ri >= ci)
    strict = same & (ri > ci)
    row_seq = _iota((tm, DK), 0) % nb
    row_seq2 = _iota((2 * tm, DK), 0) % nb
    levels = max(1, (length - 1).bit_length())

    o_heads = []
    for h in range(H_B):
        beta_c = beta[:, h:h + 1]
        bc_c = bc[:, H_B + h:H_B + h + 1]
        bc_r = bc_t[H_B + h:H_B + h + 1, :]
        dec = jnp.exp(jnp.where(incl, bc_c - bc_r, -jnp.inf))
        kk = _mm_nt(kn[h], kn[h])
        qk = _mm_nt(qn[h] * (DK ** -0.5), kn[h])
        m = beta_c * kk * jnp.where(strict, dec, 0.0)
        attn = qk * dec
        t_inv = _unit_lower_inverse([m], levels)[0]
        ebc = jnp.exp(bc_c)
        rhs = jnp.concatenate([v[:, h * DK:(h + 1) * DK] * beta_c, kn[h] * (beta_c * ebc)], axis=1)
        uw = _mm(t_inv, rhs)
        u = uw[:, 0:DK]
        w = uw[:, DK:2 * DK]
        qd = qn[h] * (DK ** -0.5) * ebc
        kd_t = kn[h].T * jnp.exp(blast_t[H_B + h:H_B + h + 1, :] - bc_r)

        wq = jnp.concatenate([w, qd], axis=0)
        acc = jnp.zeros((2 * tm, DK), F32)
        for b in range(nb):
            acc = acc + jnp.where(row_seq2 == b, _mm(wq, s0_ref[b, h]), 0.0)
        delta = u - acc[0:tm]
        o_heads.append(acc[tm:2 * tm] + _mm(attn, delta))
        for b in range(nb):
            g_last = jnp.exp(bc[tm - nb + b:tm - nb + b + 1, H_B + h:H_B + h + 1])
            s_out_ref[b, h] = g_last * s0_ref[b, h] + _mm(kd_t, jnp.where(row_seq == b, delta, 0.0))

    y_b = _gated_out_norm(jnp.concatenate(o_heads, axis=1), z, gdnn_ref)
    y_mix = jnp.concatenate([y_a, y_b], axis=1)
    y_ref[...] = _out_ffn(x, y_mix, w_o_ref, n2_ref, wg_ref, wu_ref, wd_ref, fn_ref, final)


def _layer_spec(shape):
    nd = len(shape)

    def make(layer):
        return pl.BlockSpec((None,) + tuple(shape), lambda i, _l=layer: (_l,) + (0,) * nd,
                            pipeline_mode=pl.Buffered(1))
    return make


_WEIGHT_SHAPES = (
    (D_MODEL, P_MAIN),
    (D_MODEL, LANES),
    (CONV_W, D_QKV),
    (1, LANES),
    (1, LANES),
    (1, DK),
    (G_A, GMLP_BLOCK, GMLP_BLOCK),
    None,
    (1, D_A),
    (D_MODEL, D_MODEL),
    (1, D_MODEL),
    (1, D_MODEL),
    (D_MODEL, D_FF),
    (D_MODEL, D_FF),
    (D_FF, D_MODEL),
)


def _weight_specs(layer, bias_rows):
    specs = []
    for shape in _WEIGHT_SHAPES:
        specs.append(_layer_spec(shape if shape is not None else (bias_rows, LANES))(layer))
    specs.append(pl.BlockSpec((1, D_MODEL), lambda i: (0, 0), pipeline_mode=pl.Buffered(1)))
    return specs


def _prompt_layer(x, weights, layer, final):
    seq = x.shape[0]
    tm = PROMPT_TILE
    assert seq % tm == 0
    out_shape = (
        jax.ShapeDtypeStruct((seq, D_MODEL), F32),
        jax.ShapeDtypeStruct((H_B, DK, DK), F32),
        jax.ShapeDtypeStruct((SUBLANES, D_QKV), F32),
    )
    return pl.pallas_call(
        functools.partial(_prompt_kernel, final=final),
        grid=(seq // tm,),
        in_specs=[pl.BlockSpec((tm, D_MODEL), lambda i: (i, 0))] + _weight_specs(layer, GMLP_BLOCK),
        out_specs=(
            pl.BlockSpec((tm, D_MODEL), lambda i: (i, 0)),
            pl.BlockSpec((H_B, DK, DK), lambda i: (0, 0, 0)),
            pl.BlockSpec((SUBLANES, D_QKV), lambda i: (0, 0)),
        ),
        out_shape=out_shape,
        scratch_shapes=[
            pltpu.VMEM((H_B, DK, DK), F32),
            pltpu.VMEM((SUBLANES, D_QKV), F32),
            pltpu.VMEM((PROMPT_GROUP, D_B), F32),
        ],
        compiler_params=pltpu.CompilerParams(dimension_semantics=("arbitrary",),
                                             vmem_limit_bytes=VMEM_LIMIT),
        name=f"prompt_layer{layer}",
    )(x, *weights)


def _sample_layer(x, s0, cst, weights, layer, final, length):
    nb = SAMPLE_SEQS
    tm = nb * length
    ntile = x.shape[0] // tm
    hist = (CONV_W - 1) * nb
    out_shape = (
        jax.ShapeDtypeStruct((ntile * tm, D_MODEL), F32),
        jax.ShapeDtypeStruct((ntile * nb, H_B, DK, DK), F32),
        jax.ShapeDtypeStruct((ntile, hist, D_QKV), F32),
        jax.ShapeDtypeStruct((ntile * tm, D_A), F32),
    )
    return pl.pallas_call(
        functools.partial(_sample_kernel, final=final, length=length),
        grid=(ntile,),
        in_specs=[
            pl.BlockSpec((tm, D_MODEL), lambda i: (i, 0)),
            pl.BlockSpec((None, nb, H_B, DK, DK), lambda i, _l=layer: (_l, i, 0, 0, 0)),
            pl.BlockSpec((None, None, hist, D_QKV), lambda i, _l=layer: (_l, i, 0, 0)),
        ] + _weight_specs(layer, tm),
        out_specs=(
            pl.BlockSpec((tm, D_MODEL), lambda i: (i, 0)),
            pl.BlockSpec((nb, H_B, DK, DK), lambda i: (i, 0, 0, 0)),
            pl.BlockSpec((None, hist, D_QKV), lambda i: (i, 0, 0)),
            pl.BlockSpec((tm, D_A), lambda i: (i, 0)),
        ),
        out_shape=out_shape,
        compiler_params=pltpu.CompilerParams(dimension_semantics=("arbitrary",),
                                             vmem_limit_bytes=VMEM_LIMIT),
        name=f"sample_layer{layer}",
    )(x, s0, cst, *weights)


def kernel(x_prompt, x_sample, state_gdn, state_conv, w_in, conv_w, a_log, dt_bias, gdn_norm, gmlp_ws, gmlp_bs, gmlp_norm, w_o, norm1, norm2, w_gate, w_up, w_down, final_norm):
    depth = w_in.shape[0]
    bp, seq, _ = x_prompt.shape
    nseq, length, _ = x_sample.shape
    assert bp == 1 and nseq % SAMPLE_SEQS == 0 and length % DN_CHUNK != 0
    nb = SAMPLE_SEQS
    ntile = nseq // nb

    w_ba = jnp.pad(w_in[:, :, P_MAIN:], ((0, 0), (0, 0), (0, LANES - 2 * H_B))).astype(BF16)
    lane_pad = ((0, 0), (0, 0), (H_B, LANES - 2 * H_B))
    bs_pad = jnp.pad(jnp.swapaxes(gmlp_bs, 1, 2), ((0, 0), (0, 0), (0, LANES - G_A)))

    def weights(bias):
        return (
            w_in[:, :, :P_MAIN].astype(BF16), w_ba, conv_w,
            jnp.pad(a_log[:, None, :], lane_pad), jnp.pad(dt_bias[:, None, :], lane_pad),
            gdn_norm[:, None, :], gmlp_ws, bias, gmlp_norm[:, None, :],
            w_o.astype(BF16), norm1[:, None, :], norm2[:, None, :],
            w_gate.astype(BF16), w_up.astype(BF16), w_down.astype(BF16), final_norm[None, :],
        )

    w_prompt = weights(bs_pad)
    w_sample = (w_prompt[:7] + (jnp.repeat(bs_pad[:, :length], nb, axis=1),) + w_prompt[8:])

    xs = x_sample.reshape(ntile, nb, length, D_MODEL).swapaxes(1, 2).reshape(nseq * length, D_MODEL)
    cst = (state_conv.reshape(depth, ntile, nb, CONV_W - 1, D_QKV).swapaxes(2, 3)
           .reshape(depth, ntile, (CONV_W - 1) * nb, D_QKV))
    xp = x_prompt.reshape(seq, D_MODEL)

    gdn_p, conv_p, gdn_s, conv_s, v_s = [], [], [], [], []
    for l in range(depth):
        final = l == depth - 1
        xp, s_fin, conv_tail = _prompt_layer(xp, w_prompt, l, final)
        gdn_p.append(s_fin[None])
        conv_p.append(conv_tail[None, SUBLANES - (CONV_W - 1):])
        xs, s_new, conv_new, v_a = _sample_layer(xs, state_gdn, cst, w_sample, l, final, length)
        gdn_s.append(s_new)
        conv_s.append(conv_new)
        v_s.append(v_a)

    def unorder(t, width):
        return (t.reshape(t.shape[:-2] + (ntile, length, nb, width)).swapaxes(-2, -3)
                .reshape(t.shape[:-2] + (nseq, length, width)))

    conv_s = (jnp.stack(conv_s).reshape(depth, ntile, CONV_W - 1, nb, D_QKV).swapaxes(2, 3)
              .reshape(depth, nseq, CONV_W - 1, D_QKV))
    return (xp.reshape(1, seq, D_MODEL), unorder(xs, D_MODEL), jnp.stack(gdn_p), jnp.stack(conv_p),
            jnp.stack(gdn_s), conv_s, unorder(jnp.stack(v_s), D_A))
```

```python
import functools

import jax
import jax.numpy as jnp
from jax import lax
from jax.experimental import pallas as pl
from jax.experimental.pallas import tpu as pltpu

D_MODEL = 1024
D_A = 512
G_A = 4
C_A = D_A // G_A
GMLP_BLOCK = 128
GMLP_CAUSAL = 64
D_B = 512
H_B = 4
DK = D_B // H_B
CONV_W = 4
D_QKV = 3 * D_B
DN_CHUNK = 64
D_FF = 2816
P_MAIN = 2 * D_A + 4 * D_B
EPS = 1e-6

LANES = 128
SUBLANES = 8
PROMPT_TILE = 512
PROMPT_GROUP = 256
FFN_SLAB = 256
SAMPLE_SEQS = 8
VMEM_LIMIT = 58 * 1024 * 1024

BF16 = jnp.bfloat16
F32 = jnp.float32


def _mm(a, b):
    return jnp.dot(a.astype(BF16), b.astype(BF16), preferred_element_type=F32)


def _mm_nt(a, b):
    return lax.dot_general(a.astype(BF16), b.astype(BF16), (((1,), (1,)), ((), ())),
                           preferred_element_type=F32)


def _rms(x, g):
    return x * lax.rsqrt(jnp.mean(x * x, axis=-1, keepdims=True) + EPS) * g


_GELU_C0 = 0.7978845608028654
_GELU_C1 = _GELU_C0 * 0.044715


def _gelu(x):
    hx = 0.5 * x
    return hx + hx * jnp.tanh(x * (_GELU_C0 + _GELU_C1 * (x * x)))


def _softplus(x):
    return jnp.maximum(x, 0.0) + jnp.log1p(jnp.exp(-jnp.abs(x)))


def _iota(shape, axis):
    return lax.broadcasted_iota(jnp.int32, shape, axis)


def _chunk_cumsum(g, stride, length):
    pos = (_iota(g.shape, 0) // stride) % length
    k = 1
    while k < length:
        g = g + jnp.where(pos >= k, pltpu.roll(g, k * stride, 0), 0.0)
        k *= 2
    return g


def _unit_lower_inverse(ms, levels):
    n = ms[0].shape[0]
    eye = (_iota((n, n), 0) == _iota((n, n), 1)).astype(F32)
    rs = [eye - m for m in ms]
    qs = list(ms)
    for _ in range(1, levels):
        qs = [_mm(q, q) for q in qs]
        rs = [r + _mm(r, q) for r, q in zip(rs, qs)]
    return rs


def _group_rms(y, g, width):
    outs = []
    for j in range(y.shape[1] // width):
        sl = slice(j * width, (j + 1) * width)
        outs.append(_rms(y[:, sl], g[:, sl]))
    return jnp.concatenate(outs, axis=1)


def _l2norm_heads(x):
    outs = []
    for h in range(H_B):
        xh = x[:, h * DK:(h + 1) * DK]
        outs.append(xh * lax.rsqrt(jnp.sum(xh * xh, axis=-1, keepdims=True) + EPS))
    return outs


def _gates(pba, alog_ref, dtb_ref):
    beta = jax.nn.sigmoid(pba)
    g = -jnp.exp(alog_ref[...]) * _softplus(pba + dtb_ref[...])
    return beta, g


def _gated_out_norm(o, z, gdnn_ref):
    g = jnp.concatenate([gdnn_ref[...]] * H_B, axis=1)
    return _group_rms(o, g, DK) * jax.nn.silu(z)


def _interleave(*stage_lists):
    lists = [[s if isinstance(s, tuple) else (s, 1) for s in stages] for stages in stage_lists if stages]
    totals = [sum(w for _, w in stages) for stages in lists]
    done = [0] * len(lists)
    spent = [0] * len(lists)
    for _ in range(sum(len(s) for s in lists)):
        k = min((i for i in range(len(lists)) if done[i] < len(lists[i])),
                key=lambda i: (spent[i] + lists[i][done[i]][1]) / totals[i])
        fn, weight = lists[k][done[k]]
        fn()
        done[k] += 1
        spent[k] += weight


def _proj_stages(ctx, x_ref, rows, n1_ref, w_in_ref, w_ba_ref):
    half = D_B // 2

    def norm():
        ctx["hb"] = _rms(x_ref[rows, :], n1_ref[...]).astype(BF16)

    def slab(name, lo, second):
        def run():
            part = jnp.dot(ctx["hb"], w_in_ref[:, lo:lo + half], preferred_element_type=F32)
            if second:
                ctx[name] = jnp.concatenate([ctx.pop(name + "/lo"), part], axis=1)
            else:
                ctx[name + "/lo"] = part
        return run

    def gates():
        ctx["pba"] = jnp.dot(ctx["hb"], w_ba_ref[...], preferred_element_type=F32)

    names = ("u", "v", "q", "k", "vv", "z")
    slabs = {n: [slab(n, i * D_B + s * half, s == 1) for s in range(2)] for i, n in enumerate(names)}
    return [norm], slabs["q"] + slabs["k"] + slabs["vv"] + [gates], slabs["u"] + slabs["v"] + slabs["z"]


def _mix_stages(ctx, rows, cw_ref, alog_ref, dtb_ref, gdnn_ref, ws_ref, bst_ref, gmn_ref,
                s_out_ref, conv_out_ref, s_scr, prev_scr, o_scr):
    tm = rows.size
    nchunk = tm // DN_CHUNK
    probs = [(h, pr) for h in range(H_B) for pr in range(tm // LANES)]
    ri = _iota((LANES, LANES), 0)
    ci = _iota((LANES, LANES), 1)

    def lane(h):
        return slice(H_B + h, H_B + h + 1)

    def rows_of(pr):
        return slice(pr * LANES, (pr + 1) * LANES)

    def gmlp_act():
        ctx["u_a"] = _gelu(ctx["u"])
        ctx["v_a"] = _gelu(ctx["v"])

    def gmlp_gate():
        bi = _iota((GMLP_BLOCK, GMLP_BLOCK), 0) // GMLP_CAUSAL
        bj = _iota((GMLP_BLOCK, GMLP_BLOCK), 1) // GMLP_CAUSAL
        bst = bst_ref[...]
        y_cols = []
        for g in range(G_A):
            wm = jnp.where(bj <= bi, ws_ref[g], 0.0)
            blocks = []
            for r in range(tm // GMLP_BLOCK):
                rw = slice(r * GMLP_BLOCK, (r + 1) * GMLP_BLOCK)
                cols = slice(g * C_A, (g + 1) * C_A)
                blocks.append(ctx["u_a"][rw, cols] * (_mm(wm, ctx["v_a"][rw, cols]) + bst[:, g:g + 1]))
            y_cols.append(jnp.concatenate(blocks, axis=0))
        ctx["y_gate"] = jnp.concatenate(y_cols, axis=1)

    def gmlp_norm():
        ctx["y_a"] = _group_rms(ctx["y_gate"], gmn_ref[...], C_A)

    def conv():
        nblk = tm // SUBLANES
        row = _iota((nblk, SUBLANES, D_B), 1)
        outs = []
        for i, name in enumerate(("q", "k", "vv")):
            cols = slice(i * D_B, (i + 1) * D_B)
            x = ctx[name]
            x3 = x.reshape(nblk, SUBLANES, D_B)
            prev = prev_scr[:, cols]
            acc = cw_ref[CONV_W - 1:CONV_W, cols] * x
            for s in range(1, CONV_W):
                rot = pltpu.roll(x3, s, 1)
                above = jnp.concatenate([pltpu.roll(prev, s, 0)[None], rot[:nblk - 1]], axis=0)
                shifted = jnp.where(row < s, above, rot).reshape(tm, D_B)
                acc = acc + cw_ref[CONV_W - 1 - s:CONV_W - s, cols] * shifted
            prev_scr[:, cols] = x[tm - SUBLANES:tm]
            conv_out_ref[:, cols] = x[tm - SUBLANES:tm]
            outs.append(jax.nn.silu(acc))
        ctx["qn"] = _l2norm_heads(outs[0])
        ctx["kn"] = _l2norm_heads(outs[1])
        ctx["vc"] = outs[2]

    def gates():
        beta, g = _gates(ctx["pba"], alog_ref, dtb_ref)
        bc = _chunk_cumsum(g, 1, DN_CHUNK)
        blast = jnp.concatenate(
            [jnp.broadcast_to(bc[(c + 1) * DN_CHUNK - 1:(c + 1) * DN_CHUNK, :], (DN_CHUNK, LANES))
             for c in range(nchunk)], axis=0)
        ctx.update(beta=beta, bc=bc, bc_t=bc.T, blast_t=blast.T)

    def chunk_mats():
        same = (ri // DN_CHUNK) == (ci // DN_CHUNK)
        incl = same & (ri >= ci)
        strict = same & (ri > ci)
        bc, bc_t, beta, qn, kn = ctx["bc"], ctx["bc_t"], ctx["beta"], ctx["qn"], ctx["kn"]
        ms, attn = [], []
        for h, pr in probs:
            rw = rows_of(pr)
            dec = jnp.exp(jnp.where(incl, bc[rw, lane(h)] - bc_t[lane(h), rw], -jnp.inf))
            kq = _mm_nt(jnp.concatenate([kn[h][rw], qn[h][rw] * (DK ** -0.5)], axis=0), kn[h][rw])
            ms.append(beta[rw, h:h + 1] * kq[0:LANES] * jnp.where(strict, dec, 0.0))
            attn.append(kq[LANES:2 * LANES] * dec)
        eye = (ri == ci).astype(F32)
        ctx.update(attn=attn, qs=[-m for m in ms], rs=[eye - m for m in ms])

    def inverse_level(last):
        def run():
            rs, qs = [], []
            for r, q in zip(ctx["rs"], ctx["qs"]):
                if last:
                    rs.append(r + _mm(r, q))
                else:
                    rq = _mm(jnp.concatenate([r, q], axis=0), q)
                    rs.append(r + rq[0:LANES])
                    qs.append(rq[LANES:2 * LANES])
            ctx.update(rs=rs, qs=qs)
        return run

    def first_level():
        ctx["qs"] = [_mm(q, q) for q in ctx["qs"]]

    def solve():
        bc, bc_t, beta, qn, kn = ctx["bc"], ctx["bc_t"], ctx["beta"], ctx["qn"], ctx["kn"]
        uw, qd, kd_t = [], [], []
        for (h, pr), t_i in zip(probs, ctx["rs"]):
            rw = rows_of(pr)
            beta_c = beta[rw, h:h + 1]
            ebc = jnp.exp(bc[rw, lane(h)])
            rhs = jnp.concatenate([kn[h][rw] * (beta_c * ebc), ctx["vc"][rw, h * DK:(h + 1) * DK] * beta_c],
                                  axis=1)
            uw.append(_mm(t_i, rhs))
            qd.append(qn[h][rw] * (DK ** -0.5) * ebc)
            kd_t.append(kn[h][rw].T * jnp.exp(ctx["blast_t"][lane(h), rw] - bc_t[lane(h), rw]))
        ctx.update(uw=uw, qd=qd, kd_t=kd_t)

    def chunk_terms():
        col_half = ci // DN_CHUNK
        terms = {}
        for i, (h, pr) in enumerate(probs):
            kd_t = ctx["kd_t"][i]
            lhs = jnp.concatenate([jnp.where(col_half == 0, kd_t, 0.0), jnp.where(col_half == 1, kd_t, 0.0),
                                   ctx["attn"][i]], axis=0)
            prod = _mm(lhs, ctx["uw"][i])
            for half in range(2):
                kw = prod[half * DK:(half + 1) * DK]
                aw = prod[2 * DK + half * DN_CHUNK:2 * DK + (half + 1) * DN_CHUNK]
                q_eff = ctx["qd"][i][half * DN_CHUNK:(half + 1) * DN_CHUNK] - aw[:, 0:DK]
                terms[h, 2 * pr + half] = (jnp.concatenate([q_eff, -kw[:, 0:DK]], axis=0).astype(BF16),
                                           aw[:, DK:2 * DK], kw[:, DK:2 * DK])
        ctx["terms"] = terms
        ctx["s"] = [s_scr[h] for h in range(H_B)]

    def recur(c):
        def run():
            last = (c + 1) * DN_CHUNK - 1
            for h in range(H_B):
                lhs, o_add, b_add = ctx["terms"][h, c]
                prod = _mm(lhs, ctx["s"][h])
                o_scr[c * DN_CHUNK:(c + 1) * DN_CHUNK, h * DK:(h + 1) * DK] = prod[0:DN_CHUNK] + o_add
                g_last = jnp.exp(ctx["bc"][last:last + 1, lane(h)])
                ctx["s"][h] = g_last * ctx["s"][h] + prod[DN_CHUNK:DN_CHUNK + DK] + b_add
        return run

    def finish():
        for h in range(H_B):
            s_scr[h] = ctx["s"][h]
            s_out_ref[h] = ctx["s"][h]
        y_b = _gated_out_norm(o_scr[0:tm, :], ctx["z"], gdnn_ref)
        ctx["y_mix"] = jnp.concatenate([ctx["y_a"], y_b], axis=1).astype(BF16)

    fillers = [gmlp_act, gmlp_gate, gmlp_norm]
    tail = []
    for c in range(nchunk):
        tail.append(recur(c))
        if c < len(fillers):
            tail.append(fillers[c])
    assert nchunk >= len(fillers)
    return ([(conv, 8), gates, (chunk_mats, 2), first_level] + [inverse_level(False)] * 4
            + [inverse_level(True), solve, (chunk_terms, 2)] + tail + [finish])


def _ffn_stages(ctx, x_ref, y_ref, rows, w_o_ref, n2_ref, wg_ref, wu_ref, wd_ref, fn_ref, final):
    def out_proj():
        x1 = x_ref[rows, :] + jnp.dot(ctx["y_mix"], w_o_ref[...], preferred_element_type=F32)
        ctx["h2"] = _rms(x1, n2_ref[...]).astype(BF16)
        ctx["acc"] = x1

    def gate_up(lo):
        gate = jnp.dot(ctx["h2"], wg_ref[:, lo:lo + FFN_SLAB], preferred_element_type=F32)
        up = jnp.dot(ctx["h2"], wu_ref[:, lo:lo + FFN_SLAB], preferred_element_type=F32)
        ctx["act", lo] = (jax.nn.silu(gate) * up).astype(BF16)

    def down(lo):
        ctx["acc"] = ctx["acc"] + jnp.dot(ctx.pop(("act", lo)), wd_ref[lo:lo + FFN_SLAB, :],
                                          preferred_element_type=F32)

    def slab(lo):
        def run():
            if lo < D_FF:
                gate_up(lo)
            if lo > 0:
                down(lo - FFN_SLAB)
        return run

    def store():
        x2 = ctx["acc"]
        if final:
            x2 = _rms(x2, fn_ref[...])
        y_ref[rows, :] = x2

    assert D_FF % FFN_SLAB == 0
    return [out_proj] + [slab(lo) for lo in range(0, D_FF + FFN_SLAB, FFN_SLAB)] + [store]


def _prompt_kernel(x_ref, w_in_ref, w_ba_ref, cw_ref, alog_ref, dtb_ref, gdnn_ref, ws_ref, bst_ref,
                   gmn_ref, w_o_ref, n1_ref, n2_ref, wg_ref, wu_ref, wd_ref, fn_ref,
                   y_ref, s_out_ref, conv_out_ref, s_scr, prev_scr, o_scr, *, final):
    step = pl.program_id(0)

    @pl.when(step == 0)
    def _():
        s_scr[...] = jnp.zeros_like(s_scr)
        prev_scr[...] = jnp.zeros_like(prev_scr)

    ngroup = PROMPT_TILE // PROMPT_GROUP
    proj, mix, ffn = [], [], []
    for j in range(ngroup):
        ctx = {}
        rows = pl.ds(j * PROMPT_GROUP, PROMPT_GROUP)
        proj.append(_proj_stages(ctx, x_ref, rows, n1_ref, w_in_ref, w_ba_ref))
        mix.append(_mix_stages(ctx, rows, cw_ref, alog_ref, dtb_ref, gdnn_ref, ws_ref, bst_ref, gmn_ref,
                               s_out_ref, conv_out_ref, s_scr, prev_scr, o_scr))
        ffn.append(_ffn_stages(ctx, x_ref, y_ref, rows, w_o_ref, n2_ref, wg_ref, wu_ref, wd_ref, fn_ref,
                               final))
    for norm, _, _ in proj:
        _interleave(norm)
    _interleave(proj[0][1] + (proj[0][2] if ngroup == 1 else []))
    for j in range(ngroup):
        bulk = (proj[0][2] if j == 0 and ngroup > 1 else []) + (
            proj[j + 1][1] + proj[j + 1][2] if j + 1 < ngroup else [])
        _interleave(mix[j], bulk, ffn[j - 1] if j > 0 else None)
    _interleave(ffn[ngroup - 1])


def _sample_kernel(x_ref, s0_ref, cst_ref, w_in_ref, w_ba_ref, cw_ref, alog_ref, dtb_ref, gdnn_ref,
                   ws_ref, bst_ref, gmn_ref, w_o_ref, n1_ref, n2_ref, wg_ref, wu_ref, wd_ref, fn_ref,
                   y_ref, s_out_ref, conv_out_ref, va_ref, p_scr, pba_scr, ymix_scr, *, final, length):
    nb = SAMPLE_SEQS
    tm = nb * length
    step = pl.program_id(0)

    @pl.when(step == 0)
    def _():
        hb = _rms(x_ref[...], n1_ref[...]).astype(BF16)
        for lo in range(0, P_MAIN, D_B):
            p_scr[:, lo:lo + D_B] = jnp.dot(hb, w_in_ref[:, lo:lo + D_B], preferred_element_type=F32)
        pba_scr[...] = jnp.dot(hb, w_ba_ref[...], preferred_element_type=F32)
        va_ref[...] = _gelu(p_scr[:, D_A:2 * D_A])

    rows = pl.ds(pl.multiple_of(step * tm, tm), tm)
    p = p_scr[rows, :]
    pba = pba_scr[rows, :]

    ri = _iota((tm, tm), 0)
    ci = _iota((tm, tm), 1)
    same = (ri % nb) == (ci % nb)

    u_a = _gelu(p[:, 0:D_A])
    v_a = _gelu(p[:, D_A:2 * D_A])
    expand = ((_iota((tm, length), 0) // nb) == _iota((tm, length), 1)).astype(F32)
    bst = bst_ref[...]
    y_cols = []
    for g in range(G_A):
        w_small = ws_ref[g][0:length, 0:length]
        wk = jnp.where(same, _mm_nt(_mm(expand, w_small), expand), 0.0)
        cols = slice(g * C_A, (g + 1) * C_A)
        s = _mm(wk, v_a[:, cols]) + bst[:, g:g + 1]
        y_cols.append(u_a[:, cols] * s)
    y_a = _group_rms(jnp.concatenate(y_cols, axis=1), gmn_ref[...], C_A)

    qkv = p[:, 2 * D_A:2 * D_A + D_QKV]
    hist = (CONV_W - 1) * nb
    xp = jnp.concatenate([cst_ref[...], qkv], axis=0)
    conv = cw_ref[0:1, :] * xp[0:tm]
    for i in range(1, CONV_W):
        conv = conv + cw_ref[i:i + 1, :] * xp[i * nb:i * nb + tm]
    conv_out_ref[...] = xp[tm:tm + hist]
    qkv_c = jax.nn.silu(conv)
    qn = _l2norm_heads(qkv_c[:, 0:D_B])
    kn = _l2norm_heads(qkv_c[:, D_B:2 * D_B])
    v = qkv_c[:, 2 * D_B:3 * D_B]
    z = p[:, 2 * D_A + D_QKV:P_MAIN]

    beta, g = _gates(pba, alog_ref, dtb_ref)
    bc = _chunk_cumsum(g, nb, length)
    blast = jnp.concatenate([bc[tm - nb:tm]] * length, axis=0)
    bc_t = bc.T
    blast_t = blast.T
    incl = same & (ri >= ci)
    strict = same & (ri > ci)
    row_seq = _iota((tm, DK), 0) % nb
    row_seq2 = _iota((2 * tm, DK), 0) % nb
    levels = max(1, (length - 1).bit_length())
    heads = range(H_B)

    def lane(h):
        return slice(H_B + h, H_B + h + 1)

    ms, attn = [], []
    for h in heads:
        dec = jnp.exp(jnp.where(incl, bc[:, lane(h)] - bc_t[lane(h), :], -jnp.inf))
        kq = _mm_nt(jnp.concatenate([kn[h], qn[h] * (DK ** -0.5)], axis=0), kn[h])
        ms.append(beta[:, h:h + 1] * kq[0:tm] * jnp.where(strict, dec, 0.0))
        attn.append(kq[tm:2 * tm] * dec)
    t_inv = _unit_lower_inverse(ms, levels)
    wq, u, kd_t = [], [], []
    for h, t_i in zip(heads, t_inv):
        beta_c = beta[:, h:h + 1]
        ebc = jnp.exp(bc[:, lane(h)])
        rhs = jnp.concatenate([v[:, h * DK:(h + 1) * DK] * beta_c, kn[h] * (beta_c * ebc)], axis=1)
        uw = _mm(t_i, rhs)
        u.append(uw[:, 0:DK])
        wq.append(jnp.concatenate([uw[:, DK:2 * DK], qn[h] * (DK ** -0.5) * ebc], axis=0).astype(BF16))
        kd_t.append((kn[h].T * jnp.exp(blast_t[lane(h), :] - bc_t[lane(h), :])).astype(BF16))

    acc = [jnp.zeros((2 * tm, DK), F32) for _ in heads]
    for b in range(nb):
        for h in heads:
            acc[h] = acc[h] + jnp.where(row_seq2 == b, _mm(wq[h], s0_ref[b, h]), 0.0)
    delta = [u[h] - acc[h][0:tm] for h in heads]
    o_heads = [acc[h][tm:2 * tm] + _mm(attn[h], delta[h]) for h in heads]
    for b in range(nb):
        for h in heads:
            g_last = jnp.exp(bc[tm - nb + b:tm - nb + b + 1, lane(h)])
            s_out_ref[b, h] = g_last * s0_ref[b, h] + _mm(kd_t[h], jnp.where(row_seq == b, delta[h], 0.0))

    y_b = _gated_out_norm(jnp.concatenate(o_heads, axis=1), z, gdnn_ref)
    ymix_scr[rows, :] = jnp.concatenate([y_a, y_b], axis=1).astype(BF16)

    @pl.when(step == pl.num_programs(0) - 1)
    def _():
        ctx = {"y_mix": ymix_scr[...]}
        _interleave(_ffn_stages(ctx, x_ref, y_ref, pl.ds(0, y_ref.shape[0]), w_o_ref, n2_ref, wg_ref, wu_ref,
                                wd_ref, fn_ref, final))


def _layer_spec(shape):
    nd = len(shape)

    def make(layer):
        return pl.BlockSpec((None,) + tuple(shape), lambda i, _l=layer: (_l,) + (0,) * nd,
                            pipeline_mode=pl.Buffered(1))
    return make


_WEIGHT_SHAPES = (
    (D_MODEL, P_MAIN),
    (D_MODEL, LANES),
    (CONV_W, D_QKV),
    (1, LANES),
    (1, LANES),
    (1, DK),
    (G_A, GMLP_BLOCK, GMLP_BLOCK),
    None,
    (1, D_A),
    (D_MODEL, D_MODEL),
    (1, D_MODEL),
    (1, D_MODEL),
    (D_MODEL, D_FF),
    (D_MODEL, D_FF),
    (D_FF, D_MODEL),
)


def _weight_specs(layer, bias_rows):
    specs = []
    for shape in _WEIGHT_SHAPES:
        specs.append(_layer_spec(shape if shape is not None else (bias_rows, LANES))(layer))
    specs.append(pl.BlockSpec((1, D_MODEL), lambda i: (0, 0), pipeline_mode=pl.Buffered(1)))
    return specs


def _prompt_layer(x, weights, layer, final):
    seq = x.shape[0]
    tm = PROMPT_TILE
    assert seq % tm == 0
    out_shape = (
        jax.ShapeDtypeStruct((seq, D_MODEL), F32),
        jax.ShapeDtypeStruct((H_B, DK, DK), F32),
        jax.ShapeDtypeStruct((SUBLANES, D_QKV), F32),
    )
    return pl.pallas_call(
        functools.partial(_prompt_kernel, final=final),
        grid=(seq // tm,),
        in_specs=[pl.BlockSpec((tm, D_MODEL), lambda i: (i, 0))] + _weight_specs(layer, GMLP_BLOCK),
        out_specs=(
            pl.BlockSpec((tm, D_MODEL), lambda i: (i, 0)),
            pl.BlockSpec((H_B, DK, DK), lambda i: (0, 0, 0)),
            pl.BlockSpec((SUBLANES, D_QKV), lambda i: (0, 0)),
        ),
        out_shape=out_shape,
        scratch_shapes=[
            pltpu.VMEM((H_B, DK, DK), F32),
            pltpu.VMEM((SUBLANES, D_QKV), F32),
            pltpu.VMEM((PROMPT_GROUP, D_B), F32),
        ],
        compiler_params=pltpu.CompilerParams(dimension_semantics=("arbitrary",),
                                             vmem_limit_bytes=VMEM_LIMIT),
        name=f"prompt_layer{layer}",
    )(x, *weights)


def _sample_layer(x, s0, cst, weights, layer, final, length):
    nb = SAMPLE_SEQS
    tm = nb * length
    ntile = x.shape[0] // tm
    hist = (CONV_W - 1) * nb
    out_shape = (
        jax.ShapeDtypeStruct((ntile * tm, D_MODEL), F32),
        jax.ShapeDtypeStruct((ntile * nb, H_B, DK, DK), F32),
        jax.ShapeDtypeStruct((ntile, hist, D_QKV), F32),
        jax.ShapeDtypeStruct((ntile * tm, D_A), F32),
    )
    return pl.pallas_call(
        functools.partial(_sample_kernel, final=final, length=length),
        grid=(ntile,),
        in_specs=[
            pl.BlockSpec((ntile * tm, D_MODEL), lambda i: (0, 0)),
            pl.BlockSpec((None, nb, H_B, DK, DK), lambda i, _l=layer: (_l, i, 0, 0, 0)),
            pl.BlockSpec((None, None, hist, D_QKV), lambda i, _l=layer: (_l, i, 0, 0)),
        ] + _weight_specs(layer, tm),
        out_specs=(
            pl.BlockSpec((ntile * tm, D_MODEL), lambda i: (0, 0)),
            pl.BlockSpec((nb, H_B, DK, DK), lambda i: (i, 0, 0, 0)),
            pl.BlockSpec((None, hist, D_QKV), lambda i: (i, 0, 0)),
            pl.BlockSpec((ntile * tm, D_A), lambda i: (0, 0)),
        ),
        out_shape=out_shape,
        scratch_shapes=[
            pltpu.VMEM((ntile * tm, P_MAIN), F32),
            pltpu.VMEM((ntile * tm, LANES), F32),
            pltpu.VMEM((ntile * tm, D_MODEL), BF16),
        ],
        compiler_params=pltpu.CompilerParams(dimension_semantics=("arbitrary",),
                                             vmem_limit_bytes=VMEM_LIMIT),
        name=f"sample_layer{layer}",
    )(x, s0, cst, *weights)


def kernel(x_prompt, x_sample, state_gdn, state_conv, w_in, conv_w, a_log, dt_bias, gdn_norm, gmlp_ws, gmlp_bs, gmlp_norm, w_o, norm1, norm2, w_gate, w_up, w_down, final_norm):
    depth = w_in.shape[0]
    bp, seq, _ = x_prompt.shape
    nseq, length, _ = x_sample.shape
    assert bp == 1 and nseq % SAMPLE_SEQS == 0 and length % DN_CHUNK != 0
    nb = SAMPLE_SEQS
    ntile = nseq // nb

    w_ba = jnp.pad(w_in[:, :, P_MAIN:], ((0, 0), (0, 0), (0, LANES - 2 * H_B))).astype(BF16)
    lane_pad = ((0, 0), (0, 0), (H_B, LANES - 2 * H_B))
    bs_pad = jnp.pad(jnp.swapaxes(gmlp_bs, 1, 2), ((0, 0), (0, 0), (0, LANES - G_A)))

    def weights(bias):
        return (
            w_in[:, :, :P_MAIN].astype(BF16), w_ba, conv_w,
            jnp.pad(a_log[:, None, :], lane_pad), jnp.pad(dt_bias[:, None, :], lane_pad),
            gdn_norm[:, None, :], gmlp_ws, bias, gmlp_norm[:, None, :],
            w_o.astype(BF16), norm1[:, None, :], norm2[:, None, :],
            w_gate.astype(BF16), w_up.astype(BF16), w_down.astype(BF16), final_norm[None, :],
        )

    w_prompt = weights(bs_pad)
    w_sample = (w_prompt[:7] + (jnp.repeat(bs_pad[:, :length], nb, axis=1),) + w_prompt[8:])

    xs = x_sample.reshape(ntile, nb, length, D_MODEL).swapaxes(1, 2).reshape(nseq * length, D_MODEL)
    cst = (state_conv.reshape(depth, ntile, nb, CONV_W - 1, D_QKV).swapaxes(2, 3)
           .reshape(depth, ntile, (CONV_W - 1) * nb, D_QKV))
    xp = x_prompt.reshape(seq, D_MODEL)

    gdn_p, conv_p, gdn_s, conv_s, v_s = [], [], [], [], []
    for l in range(depth):
        final = l == depth - 1
        xp, s_fin, conv_tail = _prompt_layer(xp, w_prompt, l, final)
        gdn_p.append(s_fin[None])
        conv_p.append(conv_tail[None, SUBLANES - (CONV_W - 1):])
        xs, s_new, conv_new, v_a = _sample_layer(xs, state_gdn, cst, w_sample, l, final, length)
        gdn_s.append(s_new)
        conv_s.append(conv_new)
        v_s.append(v_a)

    def unorder(t, width):
        return (t.reshape(t.shape[:-2] + (ntile, length, nb, width)).swapaxes(-2, -3)
                .reshape(t.shape[:-2] + (nseq, length, width)))

    conv_s = (jnp.stack(conv_s).reshape(depth, ntile, CONV_W - 1, nb, D_QKV).swapaxes(2, 3)
              .reshape(depth, nseq, CONV_W - 1, D_QKV))
    return (xp.reshape(1, seq, D_MODEL), unorder(xs, D_MODEL), jnp.stack(gdn_p), jnp.stack(conv_p),
            jnp.stack(gdn_s), conv_s, unorder(jnp.stack(v_s), D_A))
```

```python
import functools

import jax
import jax.numpy as jnp
from jax import lax
from jax.experimental import pallas as pl
from jax.experimental.pallas import tpu as pltpu

D_MODEL = 1024
D_A = 512
G_A = 4
C_A = D_A // G_A
GMLP_BLOCK = 128
GMLP_CAUSAL = 64
D_B = 512
H_B = 4
DK = D_B // H_B
CONV_W = 4
D_QKV = 3 * D_B
DN_CHUNK = 64
D_FF = 2816
P_MAIN = 2 * D_A + 4 * D_B
EPS = 1e-6

LANES = 128
SUBLANES = 8
PROMPT_TILE = 512
PROMPT_GROUP = 256
FFN_SLAB = 256
DOWN_RUN = 4
SAMPLE_SEQS = 8
VMEM_LIMIT = 58 * 1024 * 1024

BF16 = jnp.bfloat16
F32 = jnp.float32


def _mm(a, b):
    return jnp.dot(a.astype(BF16), b.astype(BF16), preferred_element_type=F32)


def _mm_nt(a, b):
    return lax.dot_general(a.astype(BF16), b.astype(BF16), (((1,), (1,)), ((), ())),
                           preferred_element_type=F32)


def _rms(x, g):
    return x * lax.rsqrt(jnp.mean(x * x, axis=-1, keepdims=True) + EPS) * g


_GELU_C0 = 0.7978845608028654
_GELU_C1 = _GELU_C0 * 0.044715


def _gelu(x):
    hx = 0.5 * x
    return hx + hx * jnp.tanh(x * (_GELU_C0 + _GELU_C1 * (x * x)))


def _softplus(x):
    return jnp.maximum(x, 0.0) + jnp.log1p(jnp.exp(-jnp.abs(x)))


def _iota(shape, axis):
    return lax.broadcasted_iota(jnp.int32, shape, axis)


def _chunk_cumsum(g, stride, length):
    pos = (_iota(g.shape, 0) // stride) % length
    k = 1
    while k < length:
        g = g + jnp.where(pos >= k, pltpu.roll(g, k * stride, 0), 0.0)
        k *= 2
    return g


def _unit_lower_inverse(ms, levels):
    n = ms[0].shape[0]
    eye = (_iota((n, n), 0) == _iota((n, n), 1)).astype(F32)
    rs = [eye - m for m in ms]
    qs = list(ms)
    for _ in range(1, levels):
        qs = [_mm(q, q) for q in qs]
        rs = [r + _mm(r, q) for r, q in zip(rs, qs)]
    return rs


def _group_rms(y, g, width):
    outs = []
    for j in range(y.shape[1] // width):
        sl = slice(j * width, (j + 1) * width)
        outs.append(_rms(y[:, sl], g[:, sl]))
    return jnp.concatenate(outs, axis=1)


def _l2norm_heads(x):
    outs = []
    for h in range(H_B):
        xh = x[:, h * DK:(h + 1) * DK]
        outs.append(xh * lax.rsqrt(jnp.sum(xh * xh, axis=-1, keepdims=True) + EPS))
    return outs


def _gates(pba, alog_ref, dtb_ref):
    beta = jax.nn.sigmoid(pba)
    g = -jnp.exp(alog_ref[...]) * _softplus(pba + dtb_ref[...])
    return beta, g


def _gated_out_norm(o, z, gdnn_ref):
    g = jnp.concatenate([gdnn_ref[...]] * H_B, axis=1)
    return _group_rms(o, g, DK) * jax.nn.silu(z)


def _interleave(*stage_lists):
    lists = [[s if isinstance(s, tuple) else (s, 1) for s in stages] for stages in stage_lists if stages]
    totals = [sum(w for _, w in stages) for stages in lists]
    done = [0] * len(lists)
    spent = [0] * len(lists)
    for _ in range(sum(len(s) for s in lists)):
        k = min((i for i in range(len(lists)) if done[i] < len(lists[i])),
                key=lambda i: (spent[i] + lists[i][done[i]][1]) / totals[i])
        fn, weight = lists[k][done[k]]
        fn()
        done[k] += 1
        spent[k] += weight


def _proj_stages(ctx, x_ref, rows, n1_ref, w_in_ref, w_ba_ref):
    half = D_B // 2

    def norm():
        ctx["hb"] = _rms(x_ref[rows, :], n1_ref[...]).astype(BF16)

    def slab(name, lo, second):
        def run():
            part = jnp.dot(ctx["hb"], w_in_ref[:, lo:lo + half], preferred_element_type=F32)
            if second:
                ctx[name] = jnp.concatenate([ctx.pop(name + "/lo"), part], axis=1)
            else:
                ctx[name + "/lo"] = part
        return run

    def gates():
        ctx["pba"] = jnp.dot(ctx["hb"], w_ba_ref[...], preferred_element_type=F32)

    names = ("u", "v", "q", "k", "vv", "z")
    slabs = {n: [slab(n, i * D_B + s * half, s == 1) for s in range(2)] for i, n in enumerate(names)}
    return [norm], slabs["q"] + slabs["k"] + slabs["vv"] + [gates], slabs["u"] + slabs["v"] + slabs["z"]


def _mix_stages(ctx, rows, cw_ref, alog_ref, dtb_ref, gdnn_ref, ws_ref, bst_ref, gmn_ref,
                s_out_ref, conv_out_ref, s_scr, prev_scr, o_scr):
    tm = rows.size
    nchunk = tm // DN_CHUNK
    probs = [(h, pr) for h in range(H_B) for pr in range(tm // LANES)]
    ri = _iota((LANES, LANES), 0)
    ci = _iota((LANES, LANES), 1)

    def lane(h):
        return slice(H_B + h, H_B + h + 1)

    def rows_of(pr):
        return slice(pr * LANES, (pr + 1) * LANES)

    def gmlp_act():
        ctx["u_a"] = _gelu(ctx["u"])
        ctx["v_a"] = _gelu(ctx["v"])

    def gmlp_gate():
        bi = _iota((GMLP_BLOCK, GMLP_BLOCK), 0) // GMLP_CAUSAL
        bj = _iota((GMLP_BLOCK, GMLP_BLOCK), 1) // GMLP_CAUSAL
        bst = bst_ref[...]
        y_cols = []
        for g in range(G_A):
            wm = jnp.where(bj <= bi, ws_ref[g], 0.0)
            blocks = []
            for r in range(tm // GMLP_BLOCK):
                rw = slice(r * GMLP_BLOCK, (r + 1) * GMLP_BLOCK)
                cols = slice(g * C_A, (g + 1) * C_A)
                blocks.append(ctx["u_a"][rw, cols] * (_mm(wm, ctx["v_a"][rw, cols]) + bst[:, g:g + 1]))
            y_cols.append(jnp.concatenate(blocks, axis=0))
        ctx["y_gate"] = jnp.concatenate(y_cols, axis=1)

    def gmlp_norm():
        ctx["y_a"] = _group_rms(ctx["y_gate"], gmn_ref[...], C_A)

    def conv():
        nblk = tm // SUBLANES
        row = _iota((nblk, SUBLANES, D_B), 1)
        outs = []
        for i, name in enumerate(("q", "k", "vv")):
            cols = slice(i * D_B, (i + 1) * D_B)
            x = ctx[name]
            x3 = x.reshape(nblk, SUBLANES, D_B)
            prev = prev_scr[:, cols]
            acc = cw_ref[CONV_W - 1:CONV_W, cols] * x
            for s in range(1, CONV_W):
                rot = pltpu.roll(x3, s, 1)
                above = jnp.concatenate([pltpu.roll(prev, s, 0)[None], rot[:nblk - 1]], axis=0)
                shifted = jnp.where(row < s, above, rot).reshape(tm, D_B)
                acc = acc + cw_ref[CONV_W - 1 - s:CONV_W - s, cols] * shifted
            prev_scr[:, cols] = x[tm - SUBLANES:tm]
            conv_out_ref[:, cols] = x[tm - SUBLANES:tm]
            outs.append(jax.nn.silu(acc))
        ctx["qn"] = _l2norm_heads(outs[0])
        ctx["kn"] = _l2norm_heads(outs[1])
        ctx["vc"] = outs[2]

    def gates():
        beta, g = _gates(ctx["pba"], alog_ref, dtb_ref)
        bc = _chunk_cumsum(g, 1, DN_CHUNK)
        blast = jnp.concatenate(
            [jnp.broadcast_to(bc[(c + 1) * DN_CHUNK - 1:(c + 1) * DN_CHUNK, :], (DN_CHUNK, LANES))
             for c in range(nchunk)], axis=0)
        ctx.update(beta=beta, bc=bc, bc_t=bc.T, blast_t=blast.T)

    def chunk_mats():
        same = (ri // DN_CHUNK) == (ci // DN_CHUNK)
        incl = same & (ri >= ci)
        strict = same & (ri > ci)
        bc, bc_t, beta, qn, kn = ctx["bc"], ctx["bc_t"], ctx["beta"], ctx["qn"], ctx["kn"]
        ms, attn = [], []
        for h, pr in probs:
            rw = rows_of(pr)
            dec = jnp.exp(jnp.where(incl, bc[rw, lane(h)] - bc_t[lane(h), rw], -jnp.inf))
            kq = _mm_nt(jnp.concatenate([kn[h][rw], qn[h][rw] * (DK ** -0.5)], axis=0), kn[h][rw])
            ms.append(beta[rw, h:h + 1] * kq[0:LANES] * jnp.where(strict, dec, 0.0))
            attn.append(kq[LANES:2 * LANES] * dec)
        eye = (ri == ci).astype(F32)
        ctx.update(attn=attn, qs=[-m for m in ms], rs=[eye - m for m in ms])

    def inverse_level(last):
        def run():
            rs, qs = [], []
            for r, q in zip(ctx["rs"], ctx["qs"]):
                if last:
                    rs.append(r + _mm(r, q))
                else:
                    rq = _mm(jnp.concatenate([r, q], axis=0), q)
                    rs.append(r + rq[0:LANES])
                    qs.append(rq[LANES:2 * LANES])
            ctx.update(rs=rs, qs=qs)
        return run

    def first_level():
        ctx["qs"] = [_mm(q, q) for q in ctx["qs"]]

    def solve():
        bc, bc_t, beta, qn, kn = ctx["bc"], ctx["bc_t"], ctx["beta"], ctx["qn"], ctx["kn"]
        uw, qd, kd_t = [], [], []
        for (h, pr), t_i in zip(probs, ctx["rs"]):
            rw = rows_of(pr)
            beta_c = beta[rw, h:h + 1]
            ebc = jnp.exp(bc[rw, lane(h)])
            rhs = jnp.concatenate([kn[h][rw] * (beta_c * ebc), ctx["vc"][rw, h * DK:(h + 1) * DK] * beta_c],
                                  axis=1)
            uw.append(_mm(t_i, rhs))
            qd.append(qn[h][rw] * (DK ** -0.5) * ebc)
            kd_t.append(kn[h][rw].T * jnp.exp(ctx["blast_t"][lane(h), rw] - bc_t[lane(h), rw]))
        ctx.update(uw=uw, qd=qd, kd_t=kd_t)

    def chunk_terms():
        col_half = ci // DN_CHUNK
        terms = {}
        for i, (h, pr) in enumerate(probs):
            kd_t = ctx["kd_t"][i]
            lhs = jnp.concatenate([jnp.where(col_half == 0, kd_t, 0.0), jnp.where(col_half == 1, kd_t, 0.0),
                                   ctx["attn"][i]], axis=0)
            prod = _mm(lhs, ctx["uw"][i])
            for half in range(2):
                kw = prod[half * DK:(half + 1) * DK]
                aw = prod[2 * DK + half * DN_CHUNK:2 * DK + (half + 1) * DN_CHUNK]
                q_eff = ctx["qd"][i][half * DN_CHUNK:(half + 1) * DN_CHUNK] - aw[:, 0:DK]
                terms[h, 2 * pr + half] = (jnp.concatenate([q_eff, -kw[:, 0:DK]], axis=0).astype(BF16),
                                           aw[:, DK:2 * DK], kw[:, DK:2 * DK])
        ctx["terms"] = terms
        ctx["s"] = [s_scr[h] for h in range(H_B)]

    def recur(c):
        def run():
            last = (c + 1) * DN_CHUNK - 1
            for h in range(H_B):
                lhs, o_add, b_add = ctx["terms"][h, c]
                prod = _mm(lhs, ctx["s"][h])
                o_scr[c * DN_CHUNK:(c + 1) * DN_CHUNK, h * DK:(h + 1) * DK] = prod[0:DN_CHUNK] + o_add
                g_last = jnp.exp(ctx["bc"][last:last + 1, lane(h)])
                ctx["s"][h] = g_last * ctx["s"][h] + prod[DN_CHUNK:DN_CHUNK + DK] + b_add
        return run

    def finish():
        for h in range(H_B):
            s_scr[h] = ctx["s"][h]
            s_out_ref[h] = ctx["s"][h]
        y_b = _gated_out_norm(o_scr[0:tm, :], ctx["z"], gdnn_ref)
        ctx["y_mix"] = jnp.concatenate([ctx["y_a"], y_b], axis=1).astype(BF16)

    fillers = [gmlp_act, gmlp_gate, gmlp_norm]
    tail = []
    for c in range(nchunk):
        tail.append(recur(c))
        if c < len(fillers):
            tail.append(fillers[c])
    assert nchunk >= len(fillers)
    return ([(conv, 8), gates, (chunk_mats, 2), first_level] + [inverse_level(False)] * 4
            + [inverse_level(True), solve, (chunk_terms, 2)] + tail + [finish])


def _ffn_stages(ctx, x_ref, y_ref, rows, w_o_ref, n2_ref, wg_ref, wu_ref, wd_ref, fn_ref, final):
    def out_proj():
        x1 = x_ref[rows, :] + jnp.dot(ctx["y_mix"], w_o_ref[...], preferred_element_type=F32)
        ctx["h2"] = _rms(x1, n2_ref[...]).astype(BF16)
        ctx["acc"] = x1

    nslab = D_FF // FFN_SLAB
    assert D_FF % FFN_SLAB == 0

    def gate_up(j):
        lo = j * FFN_SLAB
        gate = jnp.dot(ctx["h2"], wg_ref[:, lo:lo + FFN_SLAB], preferred_element_type=F32)
        up = jnp.dot(ctx["h2"], wu_ref[:, lo:lo + FFN_SLAB], preferred_element_type=F32)
        ctx["act", j] = (jax.nn.silu(gate) * up).astype(BF16)

    def down(first, last):
        act = jnp.concatenate([ctx.pop(("act", j)) for j in range(first, last + 1)], axis=1)
        ctx["acc"] = ctx["acc"] + jnp.dot(act, wd_ref[first * FFN_SLAB:(last + 1) * FFN_SLAB, :],
                                          preferred_element_type=F32)

    def slab(j):
        def run():
            if j < nslab:
                gate_up(j)
            if j > 0 and (j % DOWN_RUN == 0 or j == nslab):
                down((j - 1) // DOWN_RUN * DOWN_RUN, j - 1)
        return run

    def store():
        x2 = ctx["acc"]
        if final:
            x2 = _rms(x2, fn_ref[...])
        y_ref[rows, :] = x2

    return [out_proj] + [slab(j) for j in range(nslab + 1)] + [store]


def _prompt_kernel(x_ref, w_in_ref, w_ba_ref, cw_ref, alog_ref, dtb_ref, gdnn_ref, ws_ref, bst_ref,
                   gmn_ref, w_o_ref, n1_ref, n2_ref, wg_ref, wu_ref, wd_ref, fn_ref,
                   y_ref, s_out_ref, conv_out_ref, s_scr, prev_scr, o_scr, *, final):
    step = pl.program_id(0)

    @pl.when(step == 0)
    def _():
        s_scr[...] = jnp.zeros_like(s_scr)
        prev_scr[...] = jnp.zeros_like(prev_scr)

    ngroup = PROMPT_TILE // PROMPT_GROUP
    proj, mix, ffn = [], [], []
    for j in range(ngroup):
        ctx = {}
        rows = pl.ds(j * PROMPT_GROUP, PROMPT_GROUP)
        proj.append(_proj_stages(ctx, x_ref, rows, n1_ref, w_in_ref, w_ba_ref))
        mix.append(_mix_stages(ctx, rows, cw_ref, alog_ref, dtb_ref, gdnn_ref, ws_ref, bst_ref, gmn_ref,
                               s_out_ref, conv_out_ref, s_scr, prev_scr, o_scr))
        ffn.append(_ffn_stages(ctx, x_ref, y_ref, rows, w_o_ref, n2_ref, wg_ref, wu_ref, wd_ref, fn_ref,
                               final))
    for norm, _, _ in proj:
        _interleave(norm)
    _interleave(proj[0][1] + (proj[0][2] if ngroup == 1 else []))
    for j in range(ngroup):
        bulk = (proj[0][2] if j == 0 and ngroup > 1 else []) + (
            proj[j + 1][1] + proj[j + 1][2] if j + 1 < ngroup else [])
        _interleave(mix[j], bulk, ffn[j - 1] if j > 0 else None)
    _interleave(ffn[ngroup - 1])


def _sample_kernel(x_ref, s0_ref, cst_ref, w_in_ref, w_ba_ref, cw_ref, alog_ref, dtb_ref, gdnn_ref,
                   ws_ref, bst_ref, gmn_ref, w_o_ref, n1_ref, n2_ref, wg_ref, wu_ref, wd_ref, fn_ref,
                   y_ref, s_out_ref, conv_out_ref, va_ref, p_scr, pba_scr, ymix_scr, *, final, length):
    nb = SAMPLE_SEQS
    tm = nb * length
    step = pl.program_id(0)

    @pl.when(step == 0)
    def _():
        hb = _rms(x_ref[...], n1_ref[...]).astype(BF16)
        for lo in range(0, P_MAIN, D_B):
            p_scr[:, lo:lo + D_B] = jnp.dot(hb, w_in_ref[:, lo:lo + D_B], preferred_element_type=F32)
        pba_scr[...] = jnp.dot(hb, w_ba_ref[...], preferred_element_type=F32)
        va_ref[...] = _gelu(p_scr[:, D_A:2 * D_A])

    rows = pl.ds(pl.multiple_of(step * tm, tm), tm)
    p = p_scr[rows, :]
    pba = pba_scr[rows, :]

    ri = _iota((tm, tm), 0)
    ci = _iota((tm, tm), 1)
    same = (ri % nb) == (ci % nb)

    u_a = _gelu(p[:, 0:D_A])
    v_a = _gelu(p[:, D_A:2 * D_A])
    expand = ((_iota((tm, length), 0) // nb) == _iota((tm, length), 1)).astype(F32)
    bst = bst_ref[...]
    y_cols = []
    for g in range(G_A):
        w_small = ws_ref[g][0:length, 0:length]
        wk = jnp.where(same, _mm_nt(_mm(expand, w_small), expand), 0.0)
        cols = slice(g * C_A, (g + 1) * C_A)
        s = _mm(wk, v_a[:, cols]) + bst[:, g:g + 1]
        y_cols.append(u_a[:, cols] * s)
    y_a = _group_rms(jnp.concatenate(y_cols, axis=1), gmn_ref[...], C_A)

    qkv = p[:, 2 * D_A:2 * D_A + D_QKV]
    hist = (CONV_W - 1) * nb
    xp = jnp.concatenate([cst_ref[...], qkv], axis=0)
    conv = cw_ref[0:1, :] * xp[0:tm]
    for i in range(1, CONV_W):
        conv = conv + cw_ref[i:i + 1, :] * xp[i * nb:i * nb + tm]
    conv_out_ref[...] = xp[tm:tm + hist]
    qkv_c = jax.nn.silu(conv)
    qn = _l2norm_heads(qkv_c[:, 0:D_B])
    kn = _l2norm_heads(qkv_c[:, D_B:2 * D_B])
    v = qkv_c[:, 2 * D_B:3 * D_B]
    z = p[:, 2 * D_A + D_QKV:P_MAIN]

    beta, g = _gates(pba, alog_ref, dtb_ref)
    bc = _chunk_cumsum(g, nb, length)
    blast = jnp.concatenate([bc[tm - nb:tm]] * length, axis=0)
    bc_t = bc.T
    blast_t = blast.T
    incl = same & (ri >= ci)
    strict = same & (ri > ci)
    row_seq = _iota((tm, DK), 0) % nb
    row_seq2 = _iota((2 * tm, DK), 0) % nb
    levels = max(1, (length - 1).bit_length())
    heads = range(H_B)

    def lane(h):
        return slice(H_B + h, H_B + h + 1)

    ms, attn = [], []
    for h in heads:
        dec = jnp.exp(jnp.where(incl, bc[:, lane(h)] - bc_t[lane(h), :], -jnp.inf))
        kq = _mm_nt(jnp.concatenate([kn[h], qn[h] * (DK ** -0.5)], axis=0), kn[h])
        ms.append(beta[:, h:h + 1] * kq[0:tm] * jnp.where(strict, dec, 0.0))
        attn.append(kq[tm:2 * tm] * dec)
    t_inv = _unit_lower_inverse(ms, levels)
    wq, u, kd_t = [], [], []
    for h, t_i in zip(heads, t_inv):
        beta_c = beta[:, h:h + 1]
        ebc = jnp.exp(bc[:, lane(h)])
        rhs = jnp.concatenate([v[:, h * DK:(h + 1) * DK] * beta_c, kn[h] * (beta_c * ebc)], axis=1)
        uw = _mm(t_i, rhs)
        u.append(uw[:, 0:DK])
        wq.append(jnp.concatenate([uw[:, DK:2 * DK], qn[h] * (DK ** -0.5) * ebc], axis=0).astype(BF16))
        kd_t.append((kn[h].T * jnp.exp(blast_t[lane(h), :] - bc_t[lane(h), :])).astype(BF16))

    acc = [jnp.zeros((2 * tm, DK), F32) for _ in heads]
    for b in range(nb):
        for h in heads:
            acc[h] = acc[h] + jnp.where(row_seq2 == b, _mm(wq[h], s0_ref[b, h]), 0.0)
    delta = [u[h] - acc[h][0:tm] for h in heads]
    o_heads = [acc[h][tm:2 * tm] + _mm(attn[h], delta[h]) for h in heads]
    for b in range(nb):
        for h in heads:
            g_last = jnp.exp(bc[tm - nb + b:tm - nb + b + 1, lane(h)])
            s_out_ref[b, h] = g_last * s0_ref[b, h] + _mm(kd_t[h], jnp.where(row_seq == b, delta[h], 0.0))

    y_b = _gated_out_norm(jnp.concatenate(o_heads, axis=1), z, gdnn_ref)
    ymix_scr[rows, :] = jnp.concatenate([y_a, y_b], axis=1).astype(BF16)

    @pl.when(step == pl.num_programs(0) - 1)
    def _():
        ctx = {"y_mix": ymix_scr[...]}
        _interleave(_ffn_stages(ctx, x_ref, y_ref, pl.ds(0, y_ref.shape[0]), w_o_ref, n2_ref, wg_ref, wu_ref,
                                wd_ref, fn_ref, final))


def _layer_spec(shape):
    nd = len(shape)

    def make(layer):
        return pl.BlockSpec((None,) + tuple(shape), lambda i, _l=layer: (_l,) + (0,) * nd,
                            pipeline_mode=pl.Buffered(1))
    return make


_WEIGHT_SHAPES = (
    (D_MODEL, P_MAIN),
    (D_MODEL, LANES),
    (CONV_W, D_QKV),
    (1, LANES),
    (1, LANES),
    (1, DK),
    (G_A, GMLP_BLOCK, GMLP_BLOCK),
    None,
    (1, D_A),
    (D_MODEL, D_MODEL),
    (1, D_MODEL),
    (1, D_MODEL),
    (D_MODEL, D_FF),
    (D_MODEL, D_FF),
    (D_FF, D_MODEL),
)


def _weight_specs(layer, bias_rows):
    specs = []
    for shape in _WEIGHT_SHAPES:
        specs.append(_layer_spec(shape if shape is not None else (bias_rows, LANES))(layer))
    specs.append(pl.BlockSpec((1, D_MODEL), lambda i: (0, 0), pipeline_mode=pl.Buffered(1)))
    return specs


def _prompt_layer(x, weights, layer, final):
    seq = x.shape[0]
    tm = PROMPT_TILE
    assert seq % tm == 0
    out_shape = (
        jax.ShapeDtypeStruct((seq, D_MODEL), F32),
        jax.ShapeDtypeStruct((H_B, DK, DK), F32),
        jax.ShapeDtypeStruct((SUBLANES, D_QKV), F32),
    )
    return pl.pallas_call(
        functools.partial(_prompt_kernel, final=final),
        grid=(seq // tm,),
        in_specs=[pl.BlockSpec((tm, D_MODEL), lambda i: (i, 0))] + _weight_specs(layer, GMLP_BLOCK),
        out_specs=(
            pl.BlockSpec((tm, D_MODEL), lambda i: (i, 0)),
            pl.BlockSpec((H_B, DK, DK), lambda i: (0, 0, 0)),
            pl.BlockSpec((SUBLANES, D_QKV), lambda i: (0, 0)),
        ),
        out_shape=out_shape,
        scratch_shapes=[
            pltpu.VMEM((H_B, DK, DK), F32),
            pltpu.VMEM((SUBLANES, D_QKV), F32),
            pltpu.VMEM((PROMPT_GROUP, D_B), F32),
        ],
        compiler_params=pltpu.CompilerParams(dimension_semantics=("arbitrary",),
                                             vmem_limit_bytes=VMEM_LIMIT),
        name=f"prompt_layer{layer}",
    )(x, *weights)


def _sample_layer(x, s0, cst, weights, layer, final, length):
    nb = SAMPLE_SEQS
    tm = nb * length
    ntile = x.shape[0] // tm
    hist = (CONV_W - 1) * nb
    out_shape = (
        jax.ShapeDtypeStruct((ntile * tm, D_MODEL), F32),
        jax.ShapeDtypeStruct((ntile * nb, H_B, DK, DK), F32),
        jax.ShapeDtypeStruct((ntile, hist, D_QKV), F32),
        jax.ShapeDtypeStruct((ntile * tm, D_A), F32),
    )
    return pl.pallas_call(
        functools.partial(_sample_kernel, final=final, length=length),
        grid=(ntile,),
        in_specs=[
            pl.BlockSpec((ntile * tm, D_MODEL), lambda i: (0, 0)),
            pl.BlockSpec((None, nb, H_B, DK, DK), lambda i, _l=layer: (_l, i, 0, 0, 0)),
            pl.BlockSpec((None, None, hist, D_QKV), lambda i, _l=layer: (_l, i, 0, 0)),
        ] + _weight_specs(layer, tm),
        out_specs=(
            pl.BlockSpec((ntile * tm, D_MODEL), lambda i: (0, 0)),
            pl.BlockSpec((nb, H_B, DK, DK), lambda i: (i, 0, 0, 0)),
            pl.BlockSpec((None, hist, D_QKV), lambda i: (i, 0, 0)),
            pl.BlockSpec((ntile * tm, D_A), lambda i: (0, 0)),
        ),
        out_shape=out_shape,
        scratch_shapes=[
            pltpu.VMEM((ntile * tm, P_MAIN), F32),
            pltpu.VMEM((ntile * tm, LANES), F32),
            pltpu.VMEM((ntile * tm, D_MODEL), BF16),
        ],
        compiler_params=pltpu.CompilerParams(dimension_semantics=("arbitrary",),
                                             vmem_limit_bytes=VMEM_LIMIT),
        name=f"sample_layer{layer}",
    )(x, s0, cst, *weights)


def kernel(x_prompt, x_sample, state_gdn, state_conv, w_in, conv_w, a_log, dt_bias, gdn_norm, gmlp_ws, gmlp_bs, gmlp_norm, w_o, norm1, norm2, w_gate, w_up, w_down, final_norm):
    depth = w_in.shape[0]
    bp, seq, _ = x_prompt.shape
    nseq, length, _ = x_sample.shape
    assert bp == 1 and nseq % SAMPLE_SEQS == 0 and length % DN_CHUNK != 0
    nb = SAMPLE_SEQS
    ntile = nseq // nb

    w_ba = jnp.pad(w_in[:, :, P_MAIN:], ((0, 0), (0, 0), (0, LANES - 2 * H_B))).astype(BF16)
    lane_pad = ((0, 0), (0, 0), (H_B, LANES - 2 * H_B))
    bs_pad = jnp.pad(jnp.swapaxes(gmlp_bs, 1, 2), ((0, 0), (0, 0), (0, LANES - G_A)))

    def weights(bias):
        return (
            w_in[:, :, :P_MAIN].astype(BF16), w_ba, conv_w,
            jnp.pad(a_log[:, None, :], lane_pad), jnp.pad(dt_bias[:, None, :], lane_pad),
            gdn_norm[:, None, :], gmlp_ws, bias, gmlp_norm[:, None, :],
            w_o.astype(BF16), norm1[:, None, :], norm2[:, None, :],
            w_gate.astype(BF16), w_up.astype(BF16), w_down.astype(BF16), final_norm[None, :],
        )

    w_prompt = weights(bs_pad)
    w_sample = (w_prompt[:7] + (jnp.repeat(bs_pad[:, :length], nb, axis=1),) + w_prompt[8:])

    xs = x_sample.reshape(ntile, nb, length, D_MODEL).swapaxes(1, 2).reshape(nseq * length, D_MODEL)
    cst = (state_conv.reshape(depth, ntile, nb, CONV_W - 1, D_QKV).swapaxes(2, 3)
           .reshape(depth, ntile, (CONV_W - 1) * nb, D_QKV))
    xp = x_prompt.reshape(seq, D_MODEL)

    gdn_p, conv_p, gdn_s, conv_s, v_s = [], [], [], [], []
    for l in range(depth):
        final = l == depth - 1
        xp, s_fin, conv_tail = _prompt_layer(xp, w_prompt, l, final)
        gdn_p.append(s_fin[None])
        conv_p.append(conv_tail[None, SUBLANES - (CONV_W - 1):])
        xs, s_new, conv_new, v_a = _sample_layer(xs, state_gdn, cst, w_sample, l, final, length)
        gdn_s.append(s_new)
        conv_s.append(conv_new)
        v_s.append(v_a)

    def unorder(t, width):
        return (t.reshape(t.shape[:-2] + (ntile, length, nb, width)).swapaxes(-2, -3)
                .reshape(t.shape[:-2] + (nseq, length, width)))

    conv_s = (jnp.stack(conv_s).reshape(depth, ntile, CONV_W - 1, nb, D_QKV).swapaxes(2, 3)
              .reshape(depth, nseq, CONV_W - 1, D_QKV))
    return (xp.reshape(1, seq, D_MODEL), unorder(xs, D_MODEL), jnp.stack(gdn_p), jnp.stack(conv_p),
            jnp.stack(gdn_s), conv_s, unorder(jnp.stack(v_s), D_A))
```

```python
import functools

import jax
import jax.numpy as jnp
from jax import lax
from jax.experimental import pallas as pl
from jax.experimental.pallas import tpu as pltpu

D_MODEL = 1024
D_A = 512
G_A = 4
C_A = D_A // G_A
GMLP_BLOCK = 128
GMLP_CAUSAL = 64
D_B = 512
H_B = 4
DK = D_B // H_B
CONV_W = 4
D_QKV = 3 * D_B
DN_CHUNK = 64
D_FF = 2816
P_MAIN = 2 * D_A + 4 * D_B
EPS = 1e-6

LANES = 128
SUBLANES = 8
PROMPT_TILE = 512
PROMPT_GROUP = 256
FFN_SLAB = 256
SAMPLE_SEQS = 8
VMEM_LIMIT = 58 * 1024 * 1024

BF16 = jnp.bfloat16
F32 = jnp.float32


def _mm(a, b):
    return jnp.dot(a.astype(BF16), b.astype(BF16), preferred_element_type=F32)


def _mm_nt(a, b):
    return lax.dot_general(a.astype(BF16), b.astype(BF16), (((1,), (1,)), ((), ())),
                           preferred_element_type=F32)


def _rms(x, g):
    return x * lax.rsqrt(jnp.mean(x * x, axis=-1, keepdims=True) + EPS) * g


_GELU_C0 = 0.7978845608028654
_GELU_C1 = _GELU_C0 * 0.044715


def _gelu(x):
    hx = 0.5 * x
    return hx + hx * jnp.tanh(x * (_GELU_C0 + _GELU_C1 * (x * x)))


def _softplus(x):
    return jnp.maximum(x, 0.0) + jnp.log1p(jnp.exp(-jnp.abs(x)))


def _iota(shape, axis):
    return lax.broadcasted_iota(jnp.int32, shape, axis)


def _chunk_cumsum(g, stride, length):
    pos = (_iota(g.shape, 0) // stride) % length
    k = 1
    while k < length:
        g = g + jnp.where(pos >= k, pltpu.roll(g, k * stride, 0), 0.0)
        k *= 2
    return g


def _unit_lower_inverse(ms, levels):
    n = ms[0].shape[0]
    eye = (_iota((n, n), 0) == _iota((n, n), 1)).astype(F32)
    rs = [eye - m for m in ms]
    qs = list(ms)
    for _ in range(1, levels):
        qs = [_mm(q, q) for q in qs]
        rs = [r + _mm(r, q) for r, q in zip(rs, qs)]
    return rs


def _group_rms(y, g, width):
    outs = []
    for j in range(y.shape[1] // width):
        sl = slice(j * width, (j + 1) * width)
        outs.append(_rms(y[:, sl], g[:, sl]))
    return jnp.concatenate(outs, axis=1)


def _l2norm_heads(x):
    outs = []
    for h in range(H_B):
        xh = x[:, h * DK:(h + 1) * DK]
        outs.append(xh * lax.rsqrt(jnp.sum(xh * xh, axis=-1, keepdims=True) + EPS))
    return outs


def _gates(pba, alog_ref, dtb_ref):
    beta = jax.nn.sigmoid(pba)
    g = -jnp.exp(alog_ref[...]) * _softplus(pba + dtb_ref[...])
    return beta, g


def _gated_out_norm(o, z, gdnn_ref):
    g = jnp.concatenate([gdnn_ref[...]] * H_B, axis=1)
    return _group_rms(o, g, DK) * jax.nn.silu(z)


def _interleave(*stage_lists):
    lists = [[s if isinstance(s, tuple) else (s, 1) for s in stages] for stages in stage_lists if stages]
    totals = [sum(w for _, w in stages) for stages in lists]
    done = [0] * len(lists)
    spent = [0] * len(lists)
    for _ in range(sum(len(s) for s in lists)):
        k = min((i for i in range(len(lists)) if done[i] < len(lists[i])),
                key=lambda i: (spent[i] + lists[i][done[i]][1]) / totals[i])
        fn, weight = lists[k][done[k]]
        fn()
        done[k] += 1
        spent[k] += weight


def _proj_stages(ctx, x_ref, rows, n1_ref, w_in_ref, w_ba_ref):
    half = D_B // 2

    def norm():
        ctx["hb"] = _rms(x_ref[rows, :], n1_ref[...]).astype(BF16)

    def slab(name, lo, second):
        def run():
            part = jnp.dot(ctx["hb"], w_in_ref[:, lo:lo + half], preferred_element_type=F32)
            if second:
                ctx[name] = jnp.concatenate([ctx.pop(name + "/lo"), part], axis=1)
            else:
                ctx[name + "/lo"] = part
        return run

    def gates():
        ctx["pba"] = jnp.dot(ctx["hb"], w_ba_ref[...], preferred_element_type=F32)

    names = ("u", "v", "q", "k", "vv", "z")
    slabs = {n: [slab(n, i * D_B + s * half, s == 1) for s in range(2)] for i, n in enumerate(names)}
    return [norm], slabs["q"] + slabs["k"] + slabs["vv"] + [gates], slabs["u"] + slabs["v"] + slabs["z"]


def _mix_stages(ctx, rows, cw_ref, alog_ref, dtb_ref, gdnn_ref, ws_ref, bst_ref, gmn_ref,
                s_out_ref, conv_out_ref, s_scr, prev_scr, o_scr):
    tm = rows.size
    nchunk = tm // DN_CHUNK
    probs = [(h, pr) for h in range(H_B) for pr in range(tm // LANES)]
    ri = _iota((LANES, LANES), 0)
    ci = _iota((LANES, LANES), 1)

    def lane(h):
        return slice(H_B + h, H_B + h + 1)

    def rows_of(pr):
        return slice(pr * LANES, (pr + 1) * LANES)

    def gmlp_act():
        ctx["u_a"] = _gelu(ctx["u"])
        ctx["v_a"] = _gelu(ctx["v"])

    def gmlp_gate():
        bi = _iota((GMLP_BLOCK, GMLP_BLOCK), 0) // GMLP_CAUSAL
        bj = _iota((GMLP_BLOCK, GMLP_BLOCK), 1) // GMLP_CAUSAL
        bst = bst_ref[...]
        y_cols = []
        for g in range(G_A):
            wm = jnp.where(bj <= bi, ws_ref[g], 0.0)
            blocks = []
            for r in range(tm // GMLP_BLOCK):
                rw = slice(r * GMLP_BLOCK, (r + 1) * GMLP_BLOCK)
                cols = slice(g * C_A, (g + 1) * C_A)
                blocks.append(ctx["u_a"][rw, cols] * (_mm(wm, ctx["v_a"][rw, cols]) + bst[:, g:g + 1]))
            y_cols.append(jnp.concatenate(blocks, axis=0))
        ctx["y_gate"] = jnp.concatenate(y_cols, axis=1)

    def gmlp_norm():
        ctx["y_a"] = _group_rms(ctx["y_gate"], gmn_ref[...], C_A)

    def conv():
        nblk = tm // SUBLANES
        row = _iota((nblk, SUBLANES, D_B), 1)
        outs = []
        for i, name in enumerate(("q", "k", "vv")):
            cols = slice(i * D_B, (i + 1) * D_B)
            x = ctx[name]
            x3 = x.reshape(nblk, SUBLANES, D_B)
            prev = prev_scr[:, cols]
            acc = cw_ref[CONV_W - 1:CONV_W, cols] * x
            for s in range(1, CONV_W):
                rot = pltpu.roll(x3, s, 1)
                above = jnp.concatenate([pltpu.roll(prev, s, 0)[None], rot[:nblk - 1]], axis=0)
                shifted = jnp.where(row < s, above, rot).reshape(tm, D_B)
                acc = acc + cw_ref[CONV_W - 1 - s:CONV_W - s, cols] * shifted
            prev_scr[:, cols] = x[tm - SUBLANES:tm]
            conv_out_ref[:, cols] = x[tm - SUBLANES:tm]
            outs.append(jax.nn.silu(acc))
        ctx["qn"] = _l2norm_heads(outs[0])
        ctx["kn"] = _l2norm_heads(outs[1])
        ctx["vc"] = outs[2]

    def gates():
        beta, g = _gates(ctx["pba"], alog_ref, dtb_ref)
        bc = _chunk_cumsum(g, 1, DN_CHUNK)
        blast = jnp.concatenate(
            [jnp.broadcast_to(bc[(c + 1) * DN_CHUNK - 1:(c + 1) * DN_CHUNK, :], (DN_CHUNK, LANES))
             for c in range(nchunk)], axis=0)
        ctx.update(beta=beta, bc=bc, bc_t=bc.T, blast_t=blast.T)

    def chunk_mats():
        same = (ri // DN_CHUNK) == (ci // DN_CHUNK)
        incl = same & (ri >= ci)
        strict = same & (ri > ci)
        bc, bc_t, beta, qn, kn = ctx["bc"], ctx["bc_t"], ctx["beta"], ctx["qn"], ctx["kn"]
        ms, attn = [], []
        for h, pr in probs:
            rw = rows_of(pr)
            dec = jnp.exp(jnp.where(incl, bc[rw, lane(h)] - bc_t[lane(h), rw], -jnp.inf))
            kq = _mm_nt(jnp.concatenate([kn[h][rw], qn[h][rw] * (DK ** -0.5)], axis=0), kn[h][rw])
            ms.append(beta[rw, h:h + 1] * kq[0:LANES] * jnp.where(strict, dec, 0.0))
            attn.append(kq[LANES:2 * LANES] * dec)
        eye = (ri == ci).astype(F32)
        ctx.update(attn=attn, qs=[-m for m in ms], rs=[eye - m for m in ms])

    def inverse_level(last):
        def run():
            rs, qs = [], []
            for r, q in zip(ctx["rs"], ctx["qs"]):
                if last:
                    rs.append(r + _mm(r, q))
                else:
                    rq = _mm(jnp.concatenate([r, q], axis=0), q)
                    rs.append(r + rq[0:LANES])
                    qs.append(rq[LANES:2 * LANES])
            ctx.update(rs=rs, qs=qs)
        return run

    def first_level():
        ctx["qs"] = [_mm(q, q) for q in ctx["qs"]]

    def solve():
        bc, bc_t, beta, qn, kn = ctx["bc"], ctx["bc_t"], ctx["beta"], ctx["qn"], ctx["kn"]
        uw, qd, kd_t = [], [], []
        for (h, pr), t_i in zip(probs, ctx["rs"]):
            rw = rows_of(pr)
            beta_c = beta[rw, h:h + 1]
            ebc = jnp.exp(bc[rw, lane(h)])
            rhs = jnp.concatenate([kn[h][rw] * (beta_c * ebc), ctx["vc"][rw, h * DK:(h + 1) * DK] * beta_c],
                                  axis=1)
            uw.append(_mm(t_i, rhs))
            qd.append(qn[h][rw] * (DK ** -0.5) * ebc)
            kd_t.append(kn[h][rw].T * jnp.exp(ctx["blast_t"][lane(h), rw] - bc_t[lane(h), rw]))
        ctx.update(uw=uw, qd=qd, kd_t=kd_t)

    def chunk_terms():
        col_half = ci // DN_CHUNK
        terms = {}
        for i, (h, pr) in enumerate(probs):
            kd_t = ctx["kd_t"][i]
            lhs = jnp.concatenate([jnp.where(col_half == 0, kd_t, 0.0), jnp.where(col_half == 1, kd_t, 0.0),
                                   ctx["attn"][i]], axis=0)
            prod = _mm(lhs, ctx["uw"][i])
            for half in range(2):
                kw = prod[half * DK:(half + 1) * DK]
                aw = prod[2 * DK + half * DN_CHUNK:2 * DK + (half + 1) * DN_CHUNK]
                q_eff = ctx["qd"][i][half * DN_CHUNK:(half + 1) * DN_CHUNK] - aw[:, 0:DK]
                terms[h, 2 * pr + half] = (jnp.concatenate([q_eff, -kw[:, 0:DK]], axis=0).astype(BF16),
                                           aw[:, DK:2 * DK], kw[:, DK:2 * DK])
        ctx["terms"] = terms
        ctx["s"] = [s_scr[h] for h in range(H_B)]

    def recur(c):
        def run():
            last = (c + 1) * DN_CHUNK - 1
            for h in range(H_B):
                lhs, o_add, b_add = ctx["terms"][h, c]
                prod = _mm(lhs, ctx["s"][h])
                o_scr[c * DN_CHUNK:(c + 1) * DN_CHUNK, h * DK:(h + 1) * DK] = prod[0:DN_CHUNK] + o_add
                g_last = jnp.exp(ctx["bc"][last:last + 1, lane(h)])
                ctx["s"][h] = g_last * ctx["s"][h] + prod[DN_CHUNK:DN_CHUNK + DK] + b_add
        return run

    def finish():
        for h in range(H_B):
            s_scr[h] = ctx["s"][h]
            s_out_ref[h] = ctx["s"][h]
        y_b = _gated_out_norm(o_scr[0:tm, :], ctx["z"], gdnn_ref)
        ctx["y_mix"] = jnp.concatenate([ctx["y_a"], y_b], axis=1).astype(BF16)

    fillers = [gmlp_act, gmlp_gate, gmlp_norm]
    tail = []
    for c in range(nchunk):
        tail.append(recur(c))
        if c < len(fillers):
            tail.append(fillers[c])
    assert nchunk >= len(fillers)
    return ([(conv, 8), gates, (chunk_mats, 2), first_level] + [inverse_level(False)] * 4
            + [inverse_level(True), solve, (chunk_terms, 2)] + tail + [finish])


def _ffn_stages(ctx, x_ref, y_ref, rows, w_o_ref, n2_ref, wg_ref, wu_ref, wd_ref, fn_ref, final):
    def out_proj():
        x1 = x_ref[rows, :] + jnp.dot(ctx["y_mix"], w_o_ref[...], preferred_element_type=F32)
        ctx["h2"] = _rms(x1, n2_ref[...]).astype(BF16)
        ctx["acc"] = x1

    def gate_up(lo):
        gate = jnp.dot(ctx["h2"], wg_ref[:, lo:lo + FFN_SLAB], preferred_element_type=F32)
        up = jnp.dot(ctx["h2"], wu_ref[:, lo:lo + FFN_SLAB], preferred_element_type=F32)
        ctx["act", lo] = (jax.nn.silu(gate) * up).astype(BF16)

    def down(lo):
        ctx["acc"] = ctx["acc"] + jnp.dot(ctx.pop(("act", lo)), wd_ref[lo:lo + FFN_SLAB, :],
                                          preferred_element_type=F32)

    def slab(lo):
        def run():
            if lo < D_FF:
                gate_up(lo)
            if lo > 0:
                down(lo - FFN_SLAB)
        return run

    def store():
        x2 = ctx["acc"]
        if final:
            x2 = _rms(x2, fn_ref[...])
        y_ref[rows, :] = x2

    assert D_FF % FFN_SLAB == 0
    return [out_proj] + [slab(lo) for lo in range(0, D_FF + FFN_SLAB, FFN_SLAB)] + [store]


def _prompt_kernel(x_ref, w_in_ref, w_ba_ref, cw_ref, alog_ref, dtb_ref, gdnn_ref, ws_ref, bst_ref,
                   gmn_ref, w_o_ref, n1_ref, n2_ref, wg_ref, wu_ref, wd_ref, fn_ref,
                   y_ref, s_out_ref, conv_out_ref, s_scr, prev_scr, o_scr, *, final):
    step = pl.program_id(0)

    @pl.when(step == 0)
    def _():
        s_scr[...] = jnp.zeros_like(s_scr)
        prev_scr[...] = jnp.zeros_like(prev_scr)

    ngroup = PROMPT_TILE // PROMPT_GROUP
    proj, mix, ffn = [], [], []
    for j in range(ngroup):
        ctx = {}
        rows = pl.ds(j * PROMPT_GROUP, PROMPT_GROUP)
        proj.append(_proj_stages(ctx, x_ref, rows, n1_ref, w_in_ref, w_ba_ref))
        mix.append(_mix_stages(ctx, rows, cw_ref, alog_ref, dtb_ref, gdnn_ref, ws_ref, bst_ref, gmn_ref,
                               s_out_ref, conv_out_ref, s_scr, prev_scr, o_scr))
        ffn.append(_ffn_stages(ctx, x_ref, y_ref, rows, w_o_ref, n2_ref, wg_ref, wu_ref, wd_ref, fn_ref,
                               final))
    for norm, _, _ in proj:
        _interleave(norm)
    _interleave(proj[0][1] + (proj[0][2] if ngroup == 1 else []))
    for j in range(ngroup):
        bulk = (proj[0][2] if j == 0 and ngroup > 1 else []) + (
            proj[j + 1][1] + proj[j + 1][2] if j + 1 < ngroup else [])
        _interleave(mix[j], bulk, ffn[j - 1] if j > 0 else None)
    _interleave(ffn[ngroup - 1])


def _sample_kernel(x_ref, s0_ref, cst_ref, w_in_ref, w_ba_ref, cw_ref, alog_ref, dtb_ref, gdnn_ref,
                   ws_ref, bst_ref, gmn_ref, w_o_ref, n1_ref, n2_ref, wg_ref, wu_ref, wd_ref, fn_ref, _s_all_ref,
                   y_ref, s_out_ref, conv_out_ref, va_ref, p_scr, pba_scr, ymix_scr, *, final, length):
    nb = SAMPLE_SEQS
    tm = nb * length
    step = pl.program_id(0)

    @pl.when(step == 0)
    def _():
        hb = _rms(x_ref[...], n1_ref[...]).astype(BF16)
        for lo in range(0, P_MAIN, D_B):
            p_scr[:, lo:lo + D_B] = jnp.dot(hb, w_in_ref[:, lo:lo + D_B], preferred_element_type=F32)
        pba_scr[...] = jnp.dot(hb, w_ba_ref[...], preferred_element_type=F32)
        va_ref[...] = _gelu(p_scr[:, D_A:2 * D_A])

    rows = pl.ds(pl.multiple_of(step * tm, tm), tm)
    p = p_scr[rows, :]
    pba = pba_scr[rows, :]

    ri = _iota((tm, tm), 0)
    ci = _iota((tm, tm), 1)
    same = (ri % nb) == (ci % nb)

    u_a = _gelu(p[:, 0:D_A])
    v_a = _gelu(p[:, D_A:2 * D_A])
    expand = ((_iota((tm, length), 0) // nb) == _iota((tm, length), 1)).astype(F32)
    bst = bst_ref[...]
    y_cols = []
    for g in range(G_A):
        w_small = ws_ref[g][0:length, 0:length]
        wk = jnp.where(same, _mm_nt(_mm(expand, w_small), expand), 0.0)
        cols = slice(g * C_A, (g + 1) * C_A)
        s = _mm(wk, v_a[:, cols]) + bst[:, g:g + 1]
        y_cols.append(u_a[:, cols] * s)
    y_a = _group_rms(jnp.concatenate(y_cols, axis=1), gmn_ref[...], C_A)

    qkv = p[:, 2 * D_A:2 * D_A + D_QKV]
    hist = (CONV_W - 1) * nb
    xp = jnp.concatenate([cst_ref[...], qkv], axis=0)
    conv = cw_ref[0:1, :] * xp[0:tm]
    for i in range(1, CONV_W):
        conv = conv + cw_ref[i:i + 1, :] * xp[i * nb:i * nb + tm]
    conv_out_ref[...] = xp[tm:tm + hist]
    qkv_c = jax.nn.silu(conv)
    qn = _l2norm_heads(qkv_c[:, 0:D_B])
    kn = _l2norm_heads(qkv_c[:, D_B:2 * D_B])
    v = qkv_c[:, 2 * D_B:3 * D_B]
    z = p[:, 2 * D_A + D_QKV:P_MAIN]

    beta, g = _gates(pba, alog_ref, dtb_ref)
    bc = _chunk_cumsum(g, nb, length)
    blast = jnp.concatenate([bc[tm - nb:tm]] * length, axis=0)
    bc_t = bc.T
    blast_t = blast.T
    incl = same & (ri >= ci)
    strict = same & (ri > ci)
    row_seq = _iota((tm, DK), 0) % nb
    row_seq2 = _iota((2 * tm, DK), 0) % nb
    levels = max(1, (length - 1).bit_length())
    heads = range(H_B)

    def lane(h):
        return slice(H_B + h, H_B + h + 1)

    ms, attn = [], []
    for h in heads:
        dec = jnp.exp(jnp.where(incl, bc[:, lane(h)] - bc_t[lane(h), :], -jnp.inf))
        kq = _mm_nt(jnp.concatenate([kn[h], qn[h] * (DK ** -0.5)], axis=0), kn[h])
        ms.append(beta[:, h:h + 1] * kq[0:tm] * jnp.where(strict, dec, 0.0))
        attn.append(kq[tm:2 * tm] * dec)
    t_inv = _unit_lower_inverse(ms, levels)
    wq, u, kd_t = [], [], []
    for h, t_i in zip(heads, t_inv):
        beta_c = beta[:, h:h + 1]
        ebc = jnp.exp(bc[:, lane(h)])
        rhs = jnp.concatenate([v[:, h * DK:(h + 1) * DK] * beta_c, kn[h] * (beta_c * ebc)], axis=1)
        uw = _mm(t_i, rhs)
        u.append(uw[:, 0:DK])
        wq.append(jnp.concatenate([uw[:, DK:2 * DK], qn[h] * (DK ** -0.5) * ebc], axis=0).astype(BF16))
        kd_t.append((kn[h].T * jnp.exp(blast_t[lane(h), :] - bc_t[lane(h), :])).astype(BF16))

    acc = [jnp.zeros((2 * tm, DK), F32) for _ in heads]
    for b in range(nb):
        for h in heads:
            acc[h] = acc[h] + jnp.where(row_seq2 == b, _mm(wq[h], s0_ref[b, h]), 0.0)
    delta = [u[h] - acc[h][0:tm] for h in heads]
    o_heads = [acc[h][tm:2 * tm] + _mm(attn[h], delta[h]) for h in heads]
    for b in range(nb):
        for h in heads:
            g_last = jnp.exp(bc[tm - nb + b:tm - nb + b + 1, lane(h)])
            s_out_ref[b, h] = g_last * s0_ref[b, h] + _mm(kd_t[h], jnp.where(row_seq == b, delta[h], 0.0))

    y_b = _gated_out_norm(jnp.concatenate(o_heads, axis=1), z, gdnn_ref)
    ymix_scr[rows, :] = jnp.concatenate([y_a, y_b], axis=1).astype(BF16)

    @pl.when(step == pl.num_programs(0) - 1)
    def _():
        ctx = {"y_mix": ymix_scr[...]}
        _interleave(_ffn_stages(ctx, x_ref, y_ref, pl.ds(0, y_ref.shape[0]), w_o_ref, n2_ref, wg_ref, wu_ref,
                                wd_ref, fn_ref, final))


def _layer_spec(shape):
    nd = len(shape)

    def make(layer):
        return pl.BlockSpec((None,) + tuple(shape), lambda i, _l=layer: (_l,) + (0,) * nd,
                            pipeline_mode=pl.Buffered(1))
    return make


_WEIGHT_SHAPES = (
    (D_MODEL, P_MAIN),
    (D_MODEL, LANES),
    (CONV_W, D_QKV),
    (1, LANES),
    (1, LANES),
    (1, DK),
    (G_A, GMLP_BLOCK, GMLP_BLOCK),
    None,
    (1, D_A),
    (D_MODEL, D_MODEL),
    (1, D_MODEL),
    (1, D_MODEL),
    (D_MODEL, D_FF),
    (D_MODEL, D_FF),
    (D_FF, D_MODEL),
)


def _weight_specs(layer, bias_rows):
    specs = []
    for shape in _WEIGHT_SHAPES:
        specs.append(_layer_spec(shape if shape is not None else (bias_rows, LANES))(layer))
    specs.append(pl.BlockSpec((1, D_MODEL), lambda i: (0, 0), pipeline_mode=pl.Buffered(1)))
    return specs


def _prompt_layer(x, weights, layer, final):
    seq = x.shape[0]
    tm = PROMPT_TILE
    assert seq % tm == 0
    out_shape = (
        jax.ShapeDtypeStruct((seq, D_MODEL), F32),
        jax.ShapeDtypeStruct((H_B, DK, DK), F32),
        jax.ShapeDtypeStruct((SUBLANES, D_QKV), F32),
    )
    return pl.pallas_call(
        functools.partial(_prompt_kernel, final=final),
        grid=(seq // tm,),
        in_specs=[pl.BlockSpec((tm, D_MODEL), lambda i: (i, 0))] + _weight_specs(layer, GMLP_BLOCK),
        out_specs=(
            pl.BlockSpec((tm, D_MODEL), lambda i: (i, 0)),
            pl.BlockSpec((H_B, DK, DK), lambda i: (0, 0, 0)),
            pl.BlockSpec((SUBLANES, D_QKV), lambda i: (0, 0)),
        ),
        out_shape=out_shape,
        scratch_shapes=[
            pltpu.VMEM((H_B, DK, DK), F32),
            pltpu.VMEM((SUBLANES, D_QKV), F32),
            pltpu.VMEM((PROMPT_GROUP, D_B), F32),
        ],
        compiler_params=pltpu.CompilerParams(dimension_semantics=("arbitrary",),
                                             vmem_limit_bytes=VMEM_LIMIT),
        name=f"prompt_layer{layer}",
    )(x, *weights)


def _sample_layer(x, s0, cst, weights, s_all, layer, final, length):
    nb = SAMPLE_SEQS
    tm = nb * length
    ntile = x.shape[0] // tm
    hist = (CONV_W - 1) * nb
    out_shape = (
        jax.ShapeDtypeStruct((ntile * tm, D_MODEL), F32),
        jax.ShapeDtypeStruct(s_all.shape, F32),
        jax.ShapeDtypeStruct((ntile, hist, D_QKV), F32),
        jax.ShapeDtypeStruct((ntile * tm, D_A), F32),
    )
    in_specs = [
        pl.BlockSpec((ntile * tm, D_MODEL), lambda i: (0, 0)),
        pl.BlockSpec((None, nb, H_B, DK, DK), lambda i, _l=layer: (_l, i, 0, 0, 0)),
        pl.BlockSpec((None, None, hist, D_QKV), lambda i, _l=layer: (_l, i, 0, 0)),
    ] + _weight_specs(layer, tm) + [pl.BlockSpec(memory_space=pl.ANY)]
    return pl.pallas_call(
        functools.partial(_sample_kernel, final=final, length=length),
        grid=(ntile,),
        in_specs=in_specs,
        out_specs=(
            pl.BlockSpec((ntile * tm, D_MODEL), lambda i: (0, 0)),
            pl.BlockSpec((None, nb, H_B, DK, DK), lambda i, _l=layer: (_l, i, 0, 0, 0)),
            pl.BlockSpec((None, hist, D_QKV), lambda i: (i, 0, 0)),
            pl.BlockSpec((ntile * tm, D_A), lambda i: (0, 0)),
        ),
        out_shape=out_shape,
        input_output_aliases={len(in_specs) - 1: 1},
        scratch_shapes=[
            pltpu.VMEM((ntile * tm, P_MAIN), F32),
            pltpu.VMEM((ntile * tm, LANES), F32),
            pltpu.VMEM((ntile * tm, D_MODEL), BF16),
        ],
        compiler_params=pltpu.CompilerParams(dimension_semantics=("arbitrary",),
                                             vmem_limit_bytes=VMEM_LIMIT),
        name=f"sample_layer{layer}",
    )(x, s0, cst, *weights, s_all)


def kernel(x_prompt, x_sample, state_gdn, state_conv, w_in, conv_w, a_log, dt_bias, gdn_norm, gmlp_ws, gmlp_bs, gmlp_norm, w_o, norm1, norm2, w_gate, w_up, w_down, final_norm):
    depth = w_in.shape[0]
    bp, seq, _ = x_prompt.shape
    nseq, length, _ = x_sample.shape
    assert bp == 1 and nseq % SAMPLE_SEQS == 0 and length % DN_CHUNK != 0
    nb = SAMPLE_SEQS
    ntile = nseq // nb

    w_ba = jnp.pad(w_in[:, :, P_MAIN:], ((0, 0), (0, 0), (0, LANES - 2 * H_B))).astype(BF16)
    lane_pad = ((0, 0), (0, 0), (H_B, LANES - 2 * H_B))
    bs_pad = jnp.pad(jnp.swapaxes(gmlp_bs, 1, 2), ((0, 0), (0, 0), (0, LANES - G_A)))

    def weights(bias):
        return (
            w_in[:, :, :P_MAIN].astype(BF16), w_ba, conv_w,
            jnp.pad(a_log[:, None, :], lane_pad), jnp.pad(dt_bias[:, None, :], lane_pad),
            gdn_norm[:, None, :], gmlp_ws, bias, gmlp_norm[:, None, :],
            w_o.astype(BF16), norm1[:, None, :], norm2[:, None, :],
            w_gate.astype(BF16), w_up.astype(BF16), w_down.astype(BF16), final_norm[None, :],
        )

    w_prompt = weights(bs_pad)
    w_sample = (w_prompt[:7] + (jnp.repeat(bs_pad[:, :length], nb, axis=1),) + w_prompt[8:])

    xs = x_sample.reshape(ntile, nb, length, D_MODEL).swapaxes(1, 2).reshape(nseq * length, D_MODEL)
    cst = (state_conv.reshape(depth, ntile, nb, CONV_W - 1, D_QKV).swapaxes(2, 3)
           .reshape(depth, ntile, (CONV_W - 1) * nb, D_QKV))
    xp = x_prompt.reshape(seq, D_MODEL)

    gdn_p, conv_p, conv_s, v_s = [], [], [], []
    gdn_s = jnp.zeros(state_gdn.shape, F32)
    for l in range(depth):
        final = l == depth - 1
        xp, s_fin, conv_tail = _prompt_layer(xp, w_prompt, l, final)
        gdn_p.append(s_fin[None])
        conv_p.append(conv_tail[None, SUBLANES - (CONV_W - 1):])
        xs, gdn_s, conv_new, v_a = _sample_layer(xs, state_gdn, cst, w_sample, gdn_s, l, final, length)
        conv_s.append(conv_new)
        v_s.append(v_a)

    def unorder(t, width):
        return (t.reshape(t.shape[:-2] + (ntile, length, nb, width)).swapaxes(-2, -3)
                .reshape(t.shape[:-2] + (nseq, length, width)))

    conv_s = (jnp.stack(conv_s).reshape(depth, ntile, CONV_W - 1, nb, D_QKV).swapaxes(2, 3)
              .reshape(depth, nseq, CONV_W - 1, D_QKV))
    return (xp.reshape(1, seq, D_MODEL), unorder(xs, D_MODEL), jnp.stack(gdn_p), jnp.stack(conv_p),
            gdn_s, conv_s, unorder(jnp.stack(v_s), D_A))
```

```python
import functools

import jax
import jax.numpy as jnp
from jax import lax
from jax.experimental import pallas as pl
from jax.experimental.pallas import tpu as pltpu

D_MODEL = 1024
D_A = 512
G_A = 4
C_A = D_A // G_A
GMLP_BLOCK = 128
GMLP_CAUSAL = 64
D_B = 512
H_B = 4
DK = D_B // H_B
CONV_W = 4
D_QKV = 3 * D_B
DN_CHUNK = 64
D_FF = 2816
P_MAIN = 2 * D_A + 4 * D_B
EPS = 1e-6

LANES = 128
SUBLANES = 8
PROMPT_TILE = 512
PROMPT_GROUP = 256
FFN_SLAB = 256
SAMPLE_SEQS = 8
VMEM_LIMIT = 58 * 1024 * 1024

BF16 = jnp.bfloat16
F32 = jnp.float32


def _mm(a, b):
    return jnp.dot(a.astype(BF16), b.astype(BF16), preferred_element_type=F32)


def _mm_nt(a, b):
    return lax.dot_general(a.astype(BF16), b.astype(BF16), (((1,), (1,)), ((), ())),
                           preferred_element_type=F32)


def _rms(x, g):
    return x * lax.rsqrt(jnp.mean(x * x, axis=-1, keepdims=True) + EPS) * g


_GELU_C0 = 0.7978845608028654
_GELU_C1 = _GELU_C0 * 0.044715


def _gelu(x):
    hx = 0.5 * x
    return hx + hx * jnp.tanh(x * (_GELU_C0 + _GELU_C1 * (x * x)))


def _softplus(x):
    return jnp.maximum(x, 0.0) + jnp.log1p(jnp.exp(-jnp.abs(x)))


def _iota(shape, axis):
    return lax.broadcasted_iota(jnp.int32, shape, axis)


def _chunk_cumsum(g, stride, length):
    pos = (_iota(g.shape, 0) // stride) % length
    k = 1
    while k < length:
        g = g + jnp.where(pos >= k, pltpu.roll(g, k * stride, 0), 0.0)
        k *= 2
    return g


def _unit_lower_inverse(ms, levels):
    n = ms[0].shape[0]
    eye = (_iota((n, n), 0) == _iota((n, n), 1)).astype(F32)
    rs = [eye - m for m in ms]
    qs = list(ms)
    for _ in range(1, levels):
        qs = [_mm(q, q) for q in qs]
        rs = [r + _mm(r, q) for r, q in zip(rs, qs)]
    return rs


def _group_rms(y, g, width):
    outs = []
    for j in range(y.shape[1] // width):
        sl = slice(j * width, (j + 1) * width)
        outs.append(_rms(y[:, sl], g[:, sl]))
    return jnp.concatenate(outs, axis=1)


def _l2norm_heads(x):
    outs = []
    for h in range(H_B):
        xh = x[:, h * DK:(h + 1) * DK]
        outs.append(xh * lax.rsqrt(jnp.sum(xh * xh, axis=-1, keepdims=True) + EPS))
    return outs


def _gates(pba, alog_ref, dtb_ref):
    beta = jax.nn.sigmoid(pba)
    g = -jnp.exp(alog_ref[...]) * _softplus(pba + dtb_ref[...])
    return beta, g


def _gated_out_norm(o, z, gdnn_ref):
    g = jnp.concatenate([gdnn_ref[...]] * H_B, axis=1)
    return _group_rms(o, g, DK) * jax.nn.silu(z)


def _interleave(*stage_lists):
    lists = [[s if isinstance(s, tuple) else (s, 1) for s in stages] for stages in stage_lists if stages]
    totals = [sum(w for _, w in stages) for stages in lists]
    done = [0] * len(lists)
    spent = [0] * len(lists)
    for _ in range(sum(len(s) for s in lists)):
        k = min((i for i in range(len(lists)) if done[i] < len(lists[i])),
                key=lambda i: (spent[i] + lists[i][done[i]][1]) / totals[i])
        fn, weight = lists[k][done[k]]
        fn()
        done[k] += 1
        spent[k] += weight


def _proj_stages(ctx, x_ref, rows, n1_ref, w_in_ref, w_ba_ref):
    half = D_B // 2

    def norm():
        ctx["hb"] = _rms(x_ref[rows, :], n1_ref[...]).astype(BF16)

    def slab(name, lo, second):
        def run():
            part = jnp.dot(ctx["hb"], w_in_ref[:, lo:lo + half], preferred_element_type=F32)
            if second:
                ctx[name] = jnp.concatenate([ctx.pop(name + "/lo"), part], axis=1)
            else:
                ctx[name + "/lo"] = part
        return run

    def gates():
        ctx["pba"] = jnp.dot(ctx["hb"], w_ba_ref[...], preferred_element_type=F32)

    names = ("u", "v", "q", "k", "vv", "z")
    slabs = {n: [slab(n, i * D_B + s * half, s == 1) for s in range(2)] for i, n in enumerate(names)}
    return [norm], slabs["q"] + slabs["k"] + slabs["vv"] + [gates], slabs["u"] + slabs["v"] + slabs["z"]


def _mix_stages(ctx, rows, cw_ref, alog_ref, dtb_ref, gdnn_ref, ws_ref, bst_ref, gmn_ref,
                s_out_ref, conv_out_ref, s_scr, prev_scr, o_scr):
    tm = rows.size
    nchunk = tm // DN_CHUNK
    probs = [(h, pr) for h in range(H_B) for pr in range(tm // LANES)]
    ri = _iota((LANES, LANES), 0)
    ci = _iota((LANES, LANES), 1)

    def lane(h):
        return slice(H_B + h, H_B + h + 1)

    def rows_of(pr):
        return slice(pr * LANES, (pr + 1) * LANES)

    def gmlp_act():
        ctx["u_a"] = _gelu(ctx["u"])
        ctx["v_a"] = _gelu(ctx["v"])

    def gmlp_gate():
        bi = _iota((GMLP_BLOCK, GMLP_BLOCK), 0) // GMLP_CAUSAL
        bj = _iota((GMLP_BLOCK, GMLP_BLOCK), 1) // GMLP_CAUSAL
        bst = bst_ref[...]
        y_cols = []
        for g in range(G_A):
            wm = jnp.where(bj <= bi, ws_ref[g], 0.0)
            blocks = []
            for r in range(tm // GMLP_BLOCK):
                rw = slice(r * GMLP_BLOCK, (r + 1) * GMLP_BLOCK)
                cols = slice(g * C_A, (g + 1) * C_A)
                blocks.append(ctx["u_a"][rw, cols] * (_mm(wm, ctx["v_a"][rw, cols]) + bst[:, g:g + 1]))
            y_cols.append(jnp.concatenate(blocks, axis=0))
        ctx["y_gate"] = jnp.concatenate(y_cols, axis=1)

    def gmlp_norm():
        ctx["y_a"] = _group_rms(ctx["y_gate"], gmn_ref[...], C_A)

    def conv():
        nblk = tm // SUBLANES
        row = _iota((nblk, SUBLANES, D_B), 1)
        outs = []
        for i, name in enumerate(("q", "k", "vv")):
            cols = slice(i * D_B, (i + 1) * D_B)
            x = ctx[name]
            x3 = x.reshape(nblk, SUBLANES, D_B)
            prev = prev_scr[:, cols]
            acc = cw_ref[CONV_W - 1:CONV_W, cols] * x
            for s in range(1, CONV_W):
                rot = pltpu.roll(x3, s, 1)
                above = jnp.concatenate([pltpu.roll(prev, s, 0)[None], rot[:nblk - 1]], axis=0)
                shifted = jnp.where(row < s, above, rot).reshape(tm, D_B)
                acc = acc + cw_ref[CONV_W - 1 - s:CONV_W - s, cols] * shifted
            prev_scr[:, cols] = x[tm - SUBLANES:tm]
            conv_out_ref[:, cols] = x[tm - SUBLANES:tm]
            outs.append(jax.nn.silu(acc))
        ctx["qn"] = _l2norm_heads(outs[0])
        ctx["kn"] = _l2norm_heads(outs[1])
        ctx["vc"] = outs[2]

    def gates():
        beta, g = _gates(ctx["pba"], alog_ref, dtb_ref)
        bc = _chunk_cumsum(g, 1, DN_CHUNK)
        blast = jnp.concatenate(
            [jnp.broadcast_to(bc[(c + 1) * DN_CHUNK - 1:(c + 1) * DN_CHUNK, :], (DN_CHUNK, LANES))
             for c in range(nchunk)], axis=0)
        ctx.update(beta=beta, bc=bc, bc_t=bc.T, blast_t=blast.T)

    def chunk_mats():
        same = (ri // DN_CHUNK) == (ci // DN_CHUNK)
        incl = same & (ri >= ci)
        strict = same & (ri > ci)
        bc, bc_t, beta, qn, kn = ctx["bc"], ctx["bc_t"], ctx["beta"], ctx["qn"], ctx["kn"]
        ms, attn = [], []
        for h, pr in probs:
            rw = rows_of(pr)
            dec = jnp.exp(jnp.where(incl, bc[rw, lane(h)] - bc_t[lane(h), rw], -jnp.inf))
            kq = _mm_nt(jnp.concatenate([kn[h][rw], qn[h][rw] * (DK ** -0.5)], axis=0), kn[h][rw])
            ms.append(beta[rw, h:h + 1] * kq[0:LANES] * jnp.where(strict, dec, 0.0))
            attn.append(kq[LANES:2 * LANES] * dec)
        eye = (ri == ci).astype(F32)
        ctx.update(attn=attn, qs=[-m for m in ms], rs=[eye - m for m in ms])

    def inverse_level(last):
        def run():
            rs, qs = [], []
            for r, q in zip(ctx["rs"], ctx["qs"]):
                if last:
                    rs.append(r + _mm(r, q))
                else:
                    rq = _mm(jnp.concatenate([r, q], axis=0), q)
                    rs.append(r + rq[0:LANES])
                    qs.append(rq[LANES:2 * LANES])
            ctx.update(rs=rs, qs=qs)
        return run

    def first_level():
        ctx["qs"] = [_mm(q, q) for q in ctx["qs"]]

    def solve():
        bc, bc_t, beta, qn, kn = ctx["bc"], ctx["bc_t"], ctx["beta"], ctx["qn"], ctx["kn"]
        uw, qd, kd_t = [], [], []
        for (h, pr), t_i in zip(probs, ctx["rs"]):
            rw = rows_of(pr)
            beta_c = beta[rw, h:h + 1]
            ebc = jnp.exp(bc[rw, lane(h)])
            rhs = jnp.concatenate([kn[h][rw] * (beta_c * ebc), ctx["vc"][rw, h * DK:(h + 1) * DK] * beta_c],
                                  axis=1)
            uw.append(_mm(t_i, rhs))
            qd.append(qn[h][rw] * (DK ** -0.5) * ebc)
            kd_t.append(kn[h][rw].T * jnp.exp(ctx["blast_t"][lane(h), rw] - bc_t[lane(h), rw]))
        ctx.update(uw=uw, qd=qd, kd_t=kd_t)

    def chunk_terms():
        col_half = ci // DN_CHUNK
        terms = {}
        for i, (h, pr) in enumerate(probs):
            kd_t = ctx["kd_t"][i]
            lhs = jnp.concatenate([jnp.where(col_half == 0, kd_t, 0.0), jnp.where(col_half == 1, kd_t, 0.0),
                                   ctx["attn"][i]], axis=0)
            prod = _mm(lhs, ctx["uw"][i])
            for half in range(2):
                kw = prod[half * DK:(half + 1) * DK]
                aw = prod[2 * DK + half * DN_CHUNK:2 * DK + (half + 1) * DN_CHUNK]
                q_eff = ctx["qd"][i][half * DN_CHUNK:(half + 1) * DN_CHUNK] - aw[:, 0:DK]
                terms[h, 2 * pr + half] = (jnp.concatenate([q_eff, -kw[:, 0:DK]], axis=0).astype(BF16),
                                           aw[:, DK:2 * DK], kw[:, DK:2 * DK])
        ctx["terms"] = terms
        ctx["s"] = [s_scr[h] for h in range(H_B)]

    def recur(c):
        def run():
            last = (c + 1) * DN_CHUNK - 1
            for h in range(H_B):
                lhs, o_add, b_add = ctx["terms"][h, c]
                prod = _mm(lhs, ctx["s"][h])
                o_scr[c * DN_CHUNK:(c + 1) * DN_CHUNK, h * DK:(h + 1) * DK] = prod[0:DN_CHUNK] + o_add
                g_last = jnp.exp(ctx["bc"][last:last + 1, lane(h)])
                ctx["s"][h] = g_last * ctx["s"][h] + prod[DN_CHUNK:DN_CHUNK + DK] + b_add
        return run

    def finish():
        for h in range(H_B):
            s_scr[h] = ctx["s"][h]
            s_out_ref[h] = ctx["s"][h]
        y_b = _gated_out_norm(o_scr[0:tm, :], ctx["z"], gdnn_ref)
        ctx["y_mix"] = jnp.concatenate([ctx["y_a"], y_b], axis=1).astype(BF16)

    fillers = [gmlp_act, gmlp_gate, gmlp_norm]
    tail = []
    for c in range(nchunk):
        tail.append(recur(c))
        if c < len(fillers):
            tail.append(fillers[c])
    assert nchunk >= len(fillers)
    return ([(conv, 8), gates, (chunk_mats, 2), first_level] + [inverse_level(False)] * 4
            + [inverse_level(True), solve, (chunk_terms, 2)] + tail + [finish])


def _ffn_stages(ctx, x_ref, y_ref, rows, w_o_ref, n2_ref, wg_ref, wu_ref, wd_ref, fn_ref, final):
    def out_proj():
        x1 = x_ref[rows, :] + jnp.dot(ctx["y_mix"], w_o_ref[...], preferred_element_type=F32)
        ctx["h2"] = _rms(x1, n2_ref[...]).astype(BF16)
        ctx["acc"] = x1

    def gate_up(lo):
        gate = jnp.dot(ctx["h2"], wg_ref[:, lo:lo + FFN_SLAB], preferred_element_type=F32)
        up = jnp.dot(ctx["h2"], wu_ref[:, lo:lo + FFN_SLAB], preferred_element_type=F32)
        ctx["act", lo] = (jax.nn.silu(gate) * up).astype(BF16)

    def down(lo):
        ctx["acc"] = ctx["acc"] + jnp.dot(ctx.pop(("act", lo)), wd_ref[lo:lo + FFN_SLAB, :],
                                          preferred_element_type=F32)

    def slab(lo):
        def run():
            if lo < D_FF:
                gate_up(lo)
            if lo > 0:
                down(lo - FFN_SLAB)
        return run

    def store():
        x2 = ctx["acc"]
        if final:
            x2 = _rms(x2, fn_ref[...])
        y_ref[rows, :] = x2

    assert D_FF % FFN_SLAB == 0
    return [out_proj] + [slab(lo) for lo in range(0, D_FF + FFN_SLAB, FFN_SLAB)] + [store]


def _prompt_kernel(x_ref, w_in_ref, w_ba_ref, cw_ref, alog_ref, dtb_ref, gdnn_ref, ws_ref, bst_ref,
                   gmn_ref, w_o_ref, n1_ref, n2_ref, wg_ref, wu_ref, wd_ref, fn_ref,
                   y_ref, s_out_ref, conv_out_ref, s_scr, prev_scr, o_scr, *, final):
    step = pl.program_id(0)

    @pl.when(step == 0)
    def _():
        s_scr[...] = jnp.zeros_like(s_scr)
        prev_scr[...] = jnp.zeros_like(prev_scr)

    ngroup = PROMPT_TILE // PROMPT_GROUP
    proj, mix, ffn = [], [], []
    for j in range(ngroup):
        ctx = {}
        rows = pl.ds(j * PROMPT_GROUP, PROMPT_GROUP)
        proj.append(_proj_stages(ctx, x_ref, rows, n1_ref, w_in_ref, w_ba_ref))
        mix.append(_mix_stages(ctx, rows, cw_ref, alog_ref, dtb_ref, gdnn_ref, ws_ref, bst_ref, gmn_ref,
                               s_out_ref, conv_out_ref, s_scr, prev_scr, o_scr))
        ffn.append(_ffn_stages(ctx, x_ref, y_ref, rows, w_o_ref, n2_ref, wg_ref, wu_ref, wd_ref, fn_ref,
                               final))
    for norm, _, _ in proj:
        _interleave(norm)
    _interleave(proj[0][1] + (proj[0][2] if ngroup == 1 else []))
    for j in range(ngroup):
        bulk = (proj[0][2] if j == 0 and ngroup > 1 else []) + (
            proj[j + 1][1] + proj[j + 1][2] if j + 1 < ngroup else [])
        _interleave(mix[j], bulk, ffn[j - 1] if j > 0 else None)
    _interleave(ffn[ngroup - 1])


def _sample_kernel(x_ref, s0_ref, cst_ref, w_in_ref, w_ba_ref, cw_ref, alog_ref, dtb_ref, gdnn_ref,
                   ws_ref, bst_ref, gmn_ref, w_o_ref, n1_ref, n2_ref, wg_ref, wu_ref, wd_ref, fn_ref, _s_all_ref,
                   y_ref, s_out_ref, conv_out_ref, va_ref, p_scr, pba_scr, ymix_scr, *, final, length):
    nb = SAMPLE_SEQS
    tm = nb * length
    step = pl.program_id(0)

    @pl.when(step == 0)
    def _():
        hb = _rms(x_ref[...], n1_ref[...]).astype(BF16)
        for lo in range(0, P_MAIN, D_B):
            p_scr[:, lo:lo + D_B] = jnp.dot(hb, w_in_ref[:, lo:lo + D_B], preferred_element_type=F32)
        pba_scr[...] = jnp.dot(hb, w_ba_ref[...], preferred_element_type=F32)
        va_ref[...] = _gelu(p_scr[:, D_A:2 * D_A])

    rows = pl.ds(pl.multiple_of(step * tm, tm), tm)
    p = p_scr[rows, :]
    pba = pba_scr[rows, :]

    ri = _iota((tm, tm), 0)
    ci = _iota((tm, tm), 1)
    same = (ri % nb) == (ci % nb)

    u_a = _gelu(p[:, 0:D_A])
    v_a = _gelu(p[:, D_A:2 * D_A])
    expand = ((_iota((tm, length), 0) // nb) == _iota((tm, length), 1)).astype(F32)
    bst = bst_ref[...]
    y_cols = []
    for g in range(G_A):
        w_small = ws_ref[g][0:length, 0:length]
        wk = jnp.where(same, _mm_nt(_mm(expand, w_small), expand), 0.0)
        cols = slice(g * C_A, (g + 1) * C_A)
        s = _mm(wk, v_a[:, cols]) + bst[:, g:g + 1]
        y_cols.append(u_a[:, cols] * s)
    y_a = _group_rms(jnp.concatenate(y_cols, axis=1), gmn_ref[...], C_A)

    qkv = p[:, 2 * D_A:2 * D_A + D_QKV]
    hist = (CONV_W - 1) * nb
    xp = jnp.concatenate([cst_ref[...], qkv], axis=0)
    conv = cw_ref[0:1, :] * xp[0:tm]
    for i in range(1, CONV_W):
        conv = conv + cw_ref[i:i + 1, :] * xp[i * nb:i * nb + tm]
    conv_out_ref[...] = xp[tm:tm + hist]
    qkv_c = jax.nn.silu(conv)
    qn = _l2norm_heads(qkv_c[:, 0:D_B])
    kn = _l2norm_heads(qkv_c[:, D_B:2 * D_B])
    v = qkv_c[:, 2 * D_B:3 * D_B]
    z = p[:, 2 * D_A + D_QKV:P_MAIN]

    beta, g = _gates(pba, alog_ref, dtb_ref)
    bc = _chunk_cumsum(g, nb, length)
    blast = jnp.concatenate([bc[tm - nb:tm]] * length, axis=0)
    bc_t = bc.T
    blast_t = blast.T
    incl = same & (ri >= ci)
    strict = same & (ri > ci)
    row_seq = _iota((tm, DK), 0) % nb
    row_seq2 = _iota((2 * tm, DK), 0) % nb
    levels = max(1, (length - 1).bit_length())
    heads = range(H_B)

    def lane(h):
        return slice(H_B + h, H_B + h + 1)

    ms, attn = [], []
    for h in heads:
        dec = jnp.exp(jnp.where(incl, bc[:, lane(h)] - bc_t[lane(h), :], -jnp.inf))
        kq = _mm_nt(jnp.concatenate([kn[h], qn[h] * (DK ** -0.5)], axis=0), kn[h])
        ms.append(beta[:, h:h + 1] * kq[0:tm] * jnp.where(strict, dec, 0.0))
        attn.append(kq[tm:2 * tm] * dec)
    t_inv = _unit_lower_inverse(ms, levels)
    wq, u, kd_t = [], [], []
    for h, t_i in zip(heads, t_inv):
        beta_c = beta[:, h:h + 1]
        ebc = jnp.exp(bc[:, lane(h)])
        rhs = jnp.concatenate([v[:, h * DK:(h + 1) * DK] * beta_c, kn[h] * (beta_c * ebc)], axis=1)
        uw = _mm(t_i, rhs)
        u.append(uw[:, 0:DK])
        wq.append(jnp.concatenate([uw[:, DK:2 * DK], qn[h] * (DK ** -0.5) * ebc], axis=0).astype(BF16))
        kd_t.append((kn[h].T * jnp.exp(blast_t[lane(h), :] - bc_t[lane(h), :])).astype(BF16))

    acc = [jnp.zeros((2 * tm, DK), F32) for _ in heads]
    for b in range(nb):
        for h in heads:
            acc[h] = acc[h] + jnp.where(row_seq2 == b, _mm(wq[h], s0_ref[b, h]), 0.0)
    delta = [u[h] - acc[h][0:tm] for h in heads]
    o_heads = [acc[h][tm:2 * tm] + _mm(attn[h], delta[h]) for h in heads]
    for b in range(nb):
        for h in heads:
            g_last = jnp.exp(bc[tm - nb + b:tm - nb + b + 1, lane(h)])
            s_out_ref[b, h] = g_last * s0_ref[b, h] + _mm(kd_t[h], jnp.where(row_seq == b, delta[h], 0.0))

    y_b = _gated_out_norm(jnp.concatenate(o_heads, axis=1), z, gdnn_ref)
    ymix_scr[rows, :] = jnp.concatenate([y_a, y_b], axis=1).astype(BF16)

    @pl.when(step == pl.num_programs(0) - 1)
    def _():
        ctx = {"y_mix": ymix_scr[...]}
        _interleave(_ffn_stages(ctx, x_ref, y_ref, pl.ds(0, y_ref.shape[0]), w_o_ref, n2_ref, wg_ref, wu_ref,
                                wd_ref, fn_ref, final))


def _layer_spec(shape):
    nd = len(shape)

    def make(layer):
        return pl.BlockSpec((None,) + tuple(shape), lambda i, _l=layer: (_l,) + (0,) * nd,
                            pipeline_mode=pl.Buffered(1))
    return make


_WEIGHT_SHAPES = (
    (D_MODEL, P_MAIN),
    (D_MODEL, LANES),
    (CONV_W, D_QKV),
    (1, LANES),
    (1, LANES),
    (1, DK),
    (G_A, GMLP_BLOCK, GMLP_BLOCK),
    None,
    (1, D_A),
    (D_MODEL, D_MODEL),
    (1, D_MODEL),
    (1, D_MODEL),
    (D_MODEL, D_FF),
    (D_MODEL, D_FF),
    (D_FF, D_MODEL),
)


def _weight_specs(layer, bias_rows):
    specs = []
    for shape in _WEIGHT_SHAPES:
        specs.append(_layer_spec(shape if shape is not None else (bias_rows, LANES))(layer))
    specs.append(pl.BlockSpec((1, D_MODEL), lambda i: (0, 0), pipeline_mode=pl.Buffered(1)))
    return specs


def _prompt_layer(x, weights, layer, final):
    seq = x.shape[0]
    tm = PROMPT_TILE
    assert seq % tm == 0
    out_shape = (
        jax.ShapeDtypeStruct((seq, D_MODEL), F32),
        jax.ShapeDtypeStruct((H_B, DK, DK), F32),
        jax.ShapeDtypeStruct((SUBLANES, D_QKV), F32),
    )
    return pl.pallas_call(
        functools.partial(_prompt_kernel, final=final),
        grid=(seq // tm,),
        in_specs=[pl.BlockSpec((tm, D_MODEL), lambda i: (i, 0))] + _weight_specs(layer, GMLP_BLOCK),
        out_specs=(
            pl.BlockSpec((tm, D_MODEL), lambda i: (i, 0)),
            pl.BlockSpec((H_B, DK, DK), lambda i: (0, 0, 0)),
            pl.BlockSpec((SUBLANES, D_QKV), lambda i: (0, 0)),
        ),
        out_shape=out_shape,
        scratch_shapes=[
            pltpu.VMEM((H_B, DK, DK), F32),
            pltpu.VMEM((SUBLANES, D_QKV), F32),
            pltpu.VMEM((PROMPT_GROUP, D_B), F32),
        ],
        compiler_params=pltpu.CompilerParams(dimension_semantics=("arbitrary",),
                                             vmem_limit_bytes=VMEM_LIMIT),
        name=f"prompt_layer{layer}",
    )(x, *weights)


def _sample_layer(x, s0, cst, weights, s_all, layer, final, length):
    nb = SAMPLE_SEQS
    tm = nb * length
    ntile = x.shape[0] // tm
    hist = (CONV_W - 1) * nb
    out_shape = (
        jax.ShapeDtypeStruct((ntile * tm, D_MODEL), F32),
        jax.ShapeDtypeStruct(s_all.shape, F32),
        jax.ShapeDtypeStruct((ntile, hist, D_QKV), F32),
        jax.ShapeDtypeStruct((ntile * tm, D_A), F32),
    )
    in_specs = [
        pl.BlockSpec((ntile * tm, D_MODEL), lambda i: (0, 0)),
        pl.BlockSpec((None, nb, H_B, DK, DK), lambda i, _l=layer: (_l, i, 0, 0, 0)),
        pl.BlockSpec((None, None, hist, D_QKV), lambda i, _l=layer: (_l, i, 0, 0)),
    ] + _weight_specs(layer, tm) + [pl.BlockSpec(memory_space=pl.ANY)]
    return pl.pallas_call(
        functools.partial(_sample_kernel, final=final, length=length),
        grid=(ntile,),
        in_specs=in_specs,
        out_specs=(
            pl.BlockSpec((ntile * tm, D_MODEL), lambda i: (0, 0)),
            pl.BlockSpec((None, nb, H_B, DK, DK), lambda i, _l=layer: (_l, i, 0, 0, 0)),
            pl.BlockSpec((None, hist, D_QKV), lambda i: (i, 0, 0)),
            pl.BlockSpec((ntile * tm, D_A), lambda i: (0, 0)),
        ),
        out_shape=out_shape,
        input_output_aliases={len(in_specs) - 1: 1},
        scratch_shapes=[
            pltpu.VMEM((ntile * tm, P_MAIN), F32),
            pltpu.VMEM((ntile * tm, LANES), F32),
            pltpu.VMEM((ntile * tm, D_MODEL), BF16),
        ],
        compiler_params=pltpu.CompilerParams(dimension_semantics=("arbitrary",),
                                             vmem_limit_bytes=VMEM_LIMIT),
        name=f"sample_layer{layer}",
    )(x, s0, cst, *weights, s_all)


def kernel(x_prompt, x_sample, state_gdn, state_conv, w_in, conv_w, a_log, dt_bias, gdn_norm, gmlp_ws, gmlp_bs, gmlp_norm, w_o, norm1, norm2, w_gate, w_up, w_down, final_norm):
    depth = w_in.shape[0]
    bp, seq, _ = x_prompt.shape
    nseq, length, _ = x_sample.shape
    assert bp == 1 and nseq % SAMPLE_SEQS == 0 and length % DN_CHUNK != 0
    nb = SAMPLE_SEQS
    ntile = nseq // nb

    w_ba = jnp.pad(w_in[:, :, P_MAIN:], ((0, 0), (0, 0), (0, LANES - 2 * H_B))).astype(BF16)
    lane_pad = ((0, 0), (0, 0), (H_B, LANES - 2 * H_B))
    bs_pad = jnp.pad(jnp.swapaxes(gmlp_bs, 1, 2), ((0, 0), (0, 0), (0, LANES - G_A)))

    def weights(bias):
        return (
            w_in.astype(BF16), w_ba, conv_w,
            jnp.pad(a_log[:, None, :], lane_pad), jnp.pad(dt_bias[:, None, :], lane_pad),
            gdn_norm[:, None, :], gmlp_ws, bias, gmlp_norm[:, None, :],
            w_o.astype(BF16), norm1[:, None, :], norm2[:, None, :],
            w_gate.astype(BF16), w_up.astype(BF16), w_down.astype(BF16), final_norm[None, :],
        )

    w_prompt = weights(bs_pad)
    w_sample = (w_prompt[:7] + (jnp.repeat(bs_pad[:, :length], nb, axis=1),) + w_prompt[8:])

    xs = x_sample.reshape(ntile, nb, length, D_MODEL).swapaxes(1, 2).reshape(nseq * length, D_MODEL)
    cst = (state_conv.reshape(depth, ntile, nb, CONV_W - 1, D_QKV).swapaxes(2, 3)
           .reshape(depth, ntile, (CONV_W - 1) * nb, D_QKV))
    xp = x_prompt.reshape(seq, D_MODEL)

    gdn_p, conv_p, conv_s, v_s = [], [], [], []
    gdn_s = jnp.zeros(state_gdn.shape, F32)
    for l in range(depth):
        final = l == depth - 1
        xp, s_fin, conv_tail = _prompt_layer(xp, w_prompt, l, final)
        gdn_p.append(s_fin[None])
        conv_p.append(conv_tail[None, SUBLANES - (CONV_W - 1):])
        xs, gdn_s, conv_new, v_a = _sample_layer(xs, state_gdn, cst, w_sample, gdn_s, l, final, length)
        conv_s.append(conv_new)
        v_s.append(v_a)

    def unorder(t, width):
        return (t.reshape(t.shape[:-2] + (ntile, length, nb, width)).swapaxes(-2, -3)
                .reshape(t.shape[:-2] + (nseq, length, width)))

    conv_s = (jnp.stack(conv_s).reshape(depth, ntile, CONV_W - 1, nb, D_QKV).swapaxes(2, 3)
              .reshape(depth, nseq, CONV_W - 1, D_QKV))
    return (xp.reshape(1, seq, D_MODEL), unorder(xs, D_MODEL), jnp.stack(gdn_p), jnp.stack(conv_p),
            gdn_s, conv_s, unorder(jnp.stack(v_s), D_A))
```
